```python
import math
import jax, jax.numpy as jnp
from jax import lax
import numpy as np

D_MODEL = 1024
BATCH = 4
SEQ = 4096
DEPTH = 1
DEC_BATCH = 32
DEC_SEQ = 4
PAST_LEN = 16384
PAGE_SIZE = 128

HEAD_DIM = 64
MOBA_HEADS = 8
MOBA_WIDTH = MOBA_HEADS * HEAD_DIM
MOBA_BLOCK = 256
MOBA_TOPK = 3
MOBA_Q_CHUNK = 32
DIFF_HEADS = 4
DIFF_WIDTH = DIFF_HEADS * 2 * HEAD_DIM
DIFF_Q_BLOCK = 128
N_REL_HEADS = MOBA_HEADS + DIFF_HEADS
REL_BUCKETS = 32
REL_MAX_DIST = 128
FFN_HIDDEN = -(-8 * D_MODEL // (3 * 256)) * 256
IN_WIDTH = 3 * MOBA_WIDTH + 3 * DIFF_WIDTH
RMS_EPS = 1e-6

kernel_name = 'moba_diffattn_gated_merge_decoder_step'


def rms_norm(x, g):
    x32 = x.astype(jnp.float32)
    y = x32 * lax.rsqrt(jnp.mean(x32 * x32, axis=-1, keepdims=True) + RMS_EPS)
    return (y * g.astype(jnp.float32)).astype(x.dtype)


def rel_bucket(dist):
    n = jnp.maximum(dist, 0)
    exact = REL_BUCKETS // 2
    nf = jnp.maximum(n, 1).astype(jnp.float32)
    large = exact + (jnp.log(nf / exact) / math.log(REL_MAX_DIST / exact)
                     * (REL_BUCKETS - exact)).astype(jnp.int32)
    return jnp.where(n < exact, n, jnp.minimum(large, REL_BUCKETS - 1))


def split_projection(proj):
    lead = proj.shape[:-1]
    cuts = [MOBA_WIDTH, 2 * MOBA_WIDTH, 3 * MOBA_WIDTH,
            3 * MOBA_WIDTH + DIFF_WIDTH, 3 * MOBA_WIDTH + 2 * DIFF_WIDTH]
    qa, ka, va, qb, kb, vb = jnp.split(proj, cuts, axis=-1)
    a_shape = lead + (MOBA_HEADS, HEAD_DIM)
    b_shape = lead + (DIFF_HEADS, 2, HEAD_DIM)
    return (qa.reshape(a_shape), ka.reshape(a_shape), va.reshape(a_shape),
            qb.reshape(b_shape), kb.reshape(b_shape),
            vb.reshape(lead + (DIFF_HEADS, 2 * HEAD_DIM)))


def moba_attend(q, q_pos, own_k, own_v, own_pos, bias_tab, sel):
    scale = HEAD_DIM ** -0.5
    qf = q.astype(jnp.float32)
    d_own = q_pos[:, None] - own_pos[None, :]
    s_own = jnp.einsum('bqhd,bkhd->bhqk', qf, own_k.astype(jnp.float32)) * scale
    s_own = s_own + bias_tab[:, rel_bucket(d_own)][None]
    s_own = jnp.where(d_own >= 0, s_own, -jnp.inf)
    if sel is None:
        p_own = jax.nn.softmax(s_own, axis=-1)
        out = jnp.einsum('bhqk,bkhd->bqhd', p_own, own_v.astype(jnp.float32))
        return out.astype(q.dtype)
    k_sel, v_sel, sel_pos, sel_ok = sel
    B, H, Q, K, N = sel_pos.shape
    h_ix = jnp.arange(H)[None, :, None, None, None]
    s_sel = jnp.einsum('bqhd,bhqknd->bhqkn', qf, k_sel.astype(jnp.float32)) * scale
    d_sel = q_pos[None, None, :, None, None] - sel_pos
    s_sel = s_sel + bias_tab[h_ix, rel_bucket(d_sel)]
    s_sel = jnp.where(sel_ok[..., None], s_sel, -jnp.inf)
    logits = jnp.concatenate([s_sel.reshape(B, H, Q, K * N), s_own], axis=-1)
    p = jax.nn.softmax(logits, axis=-1)
    p_sel = p[..., :K * N].reshape(B, H, Q, K, N)
    p_own = p[..., K * N:]
    out = (jnp.einsum('bhqkn,bhqknd->bqhd', p_sel, v_sel.astype(jnp.float32))
           + jnp.einsum('bhqk,bkhd->bqhd', p_own, own_v.astype(jnp.float32)))
    return out.astype(q.dtype)


def moba_prompt(q, k, v, bias_tab):
    B, S, H, Dh = q.shape
    nb = -(-S // MOBA_BLOCK)
    sp = nb * MOBA_BLOCK
    pad = ((0, 0), (0, sp - S), (0, 0), (0, 0))
    kp = jnp.pad(k, pad)
    vp = jnp.pad(v, pad)
    kb = kp.reshape(B, nb, MOBA_BLOCK, H, Dh)
    vb = vp.reshape(B, nb, MOBA_BLOCK, H, Dh)
    kmean = jnp.mean(kb.astype(jnp.float32), axis=2)
    topk = min(MOBA_TOPK, nb)
    n_chunks = S // MOBA_Q_CHUNK
    qc = jnp.moveaxis(q.reshape(B, n_chunks, MOBA_Q_CHUNK, H, Dh), 1, 0)
    b_ix = jnp.arange(B)[:, None, None, None]
    h_ix = jnp.arange(H)[None, :, None, None]
    blk_ids = jnp.arange(nb)

    def chunk(args):
        qblk, c = args
        start = c * MOBA_Q_CHUNK
        q_pos = start + jnp.arange(MOBA_Q_CHUNK)
        ob = start // MOBA_BLOCK
        gate = jnp.einsum('bqhd,bnhd->bhqn', qblk.astype(jnp.float32), kmean)
        gate = jnp.where(blk_ids < ob, gate, -jnp.inf)
        _, top_idx = lax.top_k(gate, topk)
        k_sel = kb[b_ix, top_idx, :, h_ix, :]
        v_sel = vb[b_ix, top_idx, :, h_ix, :]
        sel_pos = top_idx[..., None] * MOBA_BLOCK + jnp.arange(MOBA_BLOCK)
        sel_ok = top_idx < ob
        own_start = ob * MOBA_BLOCK
        own_k = lax.dynamic_slice_in_dim(kp, own_start, MOBA_BLOCK, axis=1)
        own_v = lax.dynamic_slice_in_dim(vp, own_start, MOBA_BLOCK, axis=1)
        own_pos = own_start + jnp.arange(MOBA_BLOCK)
        return moba_attend(qblk, q_pos, own_k, own_v, own_pos, bias_tab,
                           (k_sel, v_sel, sel_pos, sel_ok))

    out = lax.map(chunk, (qc, jnp.arange(n_chunks, dtype=jnp.int32)))
    return jnp.moveaxis(out, 0, 1).reshape(B, S, H * Dh)


def gather_blocks_paged(pool, page_table, top_idx):
    B, H, Q, K = top_idx.shape
    ppb = MOBA_BLOCK // pool.shape[1]
    b_ix = jnp.arange(B)[:, None, None, None, None]
    h_ix = jnp.arange(H)[None, :, None, None, None]
    phys = page_table[b_ix, top_idx[..., None] * ppb + jnp.arange(ppb)]
    rows = pool[phys, :, h_ix, :]
    return rows.reshape(B, H, Q, K, MOBA_BLOCK, pool.shape[-1])


def moba_sample(q, k_new, v_new, pool_k, pool_v, page_table, bias_tab):
    DB, T, H, Dh = q.shape
    page = pool_k.shape[1]
    P = page_table.shape[1] * page
    nfp = P // MOBA_BLOCK
    own_start = nfp * MOBA_BLOCK
    n_own_past = P - own_start
    q_pos = P + jnp.arange(T)
    own_pages = page_table[:, own_start // page:]
    own_k = jnp.concatenate([pool_k[own_pages].reshape(DB, n_own_past, H, Dh), k_new], axis=1)
    own_v = jnp.concatenate([pool_v[own_pages].reshape(DB, n_own_past, H, Dh), v_new], axis=1)
    own_pos = own_start + jnp.arange(n_own_past + T)
    sel = None
    if nfp > 0:
        kb = pool_k[page_table[:, :nfp * (MOBA_BLOCK // page)]].reshape(DB, nfp, MOBA_BLOCK, H, Dh)
        kmean = jnp.mean(kb.astype(jnp.float32), axis=2)
        gate = jnp.einsum('bqhd,bnhd->bhqn', q.astype(jnp.float32), kmean)
        _, top_idx = lax.top_k(gate, min(MOBA_TOPK, nfp))
        k_sel = gather_blocks_paged(pool_k, page_table, top_idx)
        v_sel = gather_blocks_paged(pool_v, page_table, top_idx)
        sel_pos = top_idx[..., None] * MOBA_BLOCK + jnp.arange(MOBA_BLOCK)
        sel_ok = jnp.ones(top_idx.shape, dtype=bool)
        sel = (k_sel, v_sel, sel_pos, sel_ok)
    out = moba_attend(q, q_pos, own_k, own_v, own_pos, bias_tab, sel)
    return out.reshape(DB, T, H * Dh)


def diff_attend(q, q_pos, segments, bias_tab, lam, subln_w, lambda_init):
    scale = HEAD_DIM ** -0.5
    qf = q.astype(jnp.float32)
    logits = []
    for k, _, k_pos in segments:
        d = q_pos[:, None] - k_pos[None, :]
        s = jnp.einsum('bqhmd,bkhmd->bhmqk', qf, k.astype(jnp.float32)) * scale
        s = s + bias_tab[:, rel_bucket(d)][None, :, None]
        logits.append(jnp.where(d >= 0, s, -jnp.inf))
    p = jax.nn.softmax(jnp.concatenate(logits, axis=-1), axis=-1)
    a = p[:, :, 0] - lam * p[:, :, 1]
    outs = []
    off = 0
    for k, v, _ in segments:
        L = k.shape[1]
        outs.append(jnp.einsum('bhqk,bkhe->bqhe', a[..., off:off + L], v.astype(jnp.float32)))
        off += L
    out = sum(outs)
    out = out * lax.rsqrt(jnp.mean(out * out, axis=-1, keepdims=True) + RMS_EPS)
    out = out * subln_w.astype(jnp.float32) * (1.0 - lambda_init)
    return out.astype(q.dtype)


def diff_prompt(q, k, v, bias_tab, lam, subln_w, lambda_init):
    B, S, H, _, Dh = q.shape
    nq = S // DIFF_Q_BLOCK
    qb = jnp.moveaxis(q.reshape(B, nq, DIFF_Q_BLOCK, H, 2, Dh), 1, 0)
    k_pos = jnp.arange(S)

    def blk(args):
        qq, c = args
        q_pos = c * DIFF_Q_BLOCK + jnp.arange(DIFF_Q_BLOCK)
        return diff_attend(qq, q_pos, [(k, v, k_pos)], bias_tab, lam, subln_w, lambda_init)

    out = lax.map(blk, (qb, jnp.arange(nq, dtype=jnp.int32)))
    return jnp.moveaxis(out, 0, 1).reshape(B, S, H * 2 * Dh)


def diff_sample(q, k_new, v_new, pool_k, pool_v, page_table, bias_tab, lam, subln_w, lambda_init):
    DB, T, H, _, Dh = q.shape
    P = page_table.shape[1] * pool_k.shape[1]
    k_past = pool_k[page_table].reshape(DB, P, H, 2, Dh)
    v_past = pool_v[page_table].reshape(DB, P, H, 2 * Dh)
    segs = [(k_past, v_past, jnp.arange(P)), (k_new, v_new, P + jnp.arange(T))]
    out = diff_attend(q, P + jnp.arange(T), segs, bias_tab, lam, subln_w, lambda_init)
    return out.reshape(DB, T, H * 2 * Dh)


def merge_and_ffn(x, xn, o_a, o_b, w_merge_gate, b_merge_gate, w_branch_a, w_branch_b, w_out,
                  g_ffn, w_ffn_up, w_ffn_down):
    gates = jax.nn.sigmoid((xn @ w_merge_gate + b_merge_gate).astype(jnp.float32)).astype(x.dtype)
    merged = gates[..., :D_MODEL] * (o_a @ w_branch_a) + gates[..., D_MODEL:] * (o_b @ w_branch_b)
    h = x + merged @ w_out
    gu = rms_norm(h, g_ffn) @ w_ffn_up
    return h + (jax.nn.silu(gu[..., :FFN_HIDDEN]) * gu[..., FFN_HIDDEN:]) @ w_ffn_down


def setup_inputs(seed: int = 0) -> dict:
    key = jax.random.key(seed)
    ks = jax.random.split(key, 24)
    n_pages = PAST_LEN // PAGE_SIZE
    n_pool = (DEC_BATCH * n_pages * 5) // 4

    def nrm(k, shape, scale):
        return jax.random.normal(k, shape, jnp.float32) * scale

    page_table = jax.random.permutation(ks[6], n_pool)[:DEC_BATCH * n_pages]
    page_table = page_table.reshape(DEC_BATCH, n_pages).astype(jnp.int32)
    return {
        'x_prompt': nrm(ks[0], (BATCH, SEQ, D_MODEL), 1.0),
        'x_sample': nrm(ks[1], (DEC_BATCH, DEC_SEQ, D_MODEL), 1.0),
        'cache_k_a': nrm(ks[2], (DEPTH, n_pool, PAGE_SIZE, MOBA_HEADS, HEAD_DIM), 1.0),
        'cache_v_a': nrm(ks[3], (DEPTH, n_pool, PAGE_SIZE, MOBA_HEADS, HEAD_DIM), 1.0),
        'cache_k_b': nrm(ks[4], (DEPTH, n_pool, PAGE_SIZE, DIFF_HEADS, 2 * HEAD_DIM), 1.0),
        'cache_v_b': nrm(ks[5], (DEPTH, n_pool, PAGE_SIZE, DIFF_HEADS, 2 * HEAD_DIM), 1.0),
        'page_table': page_table,
        'g_mix': 1.0 + nrm(ks[7], (DEPTH, D_MODEL), 0.02),
        'w_in': nrm(ks[8], (DEPTH, D_MODEL, IN_WIDTH), D_MODEL ** -0.5),
        'w_merge_gate': nrm(ks[9], (DEPTH, D_MODEL, 2 * D_MODEL), D_MODEL ** -0.5),
        'b_merge_gate': nrm(ks[10], (DEPTH, 2 * D_MODEL), 0.02),
        'w_branch_a': nrm(ks[11], (DEPTH, MOBA_WIDTH, D_MODEL), MOBA_WIDTH ** -0.5),
        'w_branch_b': nrm(ks[12], (DEPTH, DIFF_WIDTH, D_MODEL), DIFF_WIDTH ** -0.5),
        'w_out': nrm(ks[13], (DEPTH, D_MODEL, D_MODEL), D_MODEL ** -0.5),
        'lambda_q1': nrm(ks[14], (DEPTH, HEAD_DIM), 0.1),
        'lambda_k1': nrm(ks[15], (DEPTH, HEAD_DIM), 0.1),
        'lambda_q2': nrm(ks[16], (DEPTH, HEAD_DIM), 0.1),
        'lambda_k2': nrm(ks[17], (DEPTH, HEAD_DIM), 0.1),
        'subln_w': 1.0 + nrm(ks[18], (DEPTH, 2 * HEAD_DIM), 0.02),
        'g_ffn': 1.0 + nrm(ks[19], (DEPTH, D_MODEL), 0.02),
        'w_ffn_up': nrm(ks[20], (DEPTH, D_MODEL, 2 * FFN_HIDDEN), D_MODEL ** -0.5),
        'w_ffn_down': nrm(ks[21], (DEPTH, FFN_HIDDEN, D_MODEL), FFN_HIDDEN ** -0.5),
        'rel_bias': nrm(ks[22], (REL_BUCKETS, N_REL_HEADS), 0.2),
        'g_final': 1.0 + nrm(ks[23], (D_MODEL,), 0.02),
    }


def reference(x_prompt, x_sample, cache_k_a, cache_v_a, cache_k_b, cache_v_b, page_table,
              g_mix, w_in, w_merge_gate, b_merge_gate, w_branch_a, w_branch_b, w_out,
              lambda_q1, lambda_k1, lambda_q2, lambda_k2, subln_w, g_ffn, w_ffn_up, w_ffn_down,
              rel_bias, g_final):
    tab = rel_bias.astype(jnp.float32).T
    tab_a = tab[:MOBA_HEADS]
    tab_b = tab[MOBA_HEADS:]
    hp = x_prompt
    hs = x_sample
    ka_p, va_p, kb_p, vb_p = [], [], [], []
    ka_s, va_s, kb_s, vb_s = [], [], [], []
    for l in range(DEPTH):
        lambda_init = 0.8 - 0.6 * math.exp(-0.3 * l)
        lam = (jnp.exp(jnp.sum(lambda_q1[l].astype(jnp.float32) * lambda_k1[l].astype(jnp.float32)))
               - jnp.exp(jnp.sum(lambda_q2[l].astype(jnp.float32) * lambda_k2[l].astype(jnp.float32)))
               + lambda_init)
        xn = rms_norm(hp, g_mix[l])
        qa, ka, va, qb, kb, vb = split_projection(xn @ w_in[l])
        o_a = moba_prompt(qa, ka, va, tab_a)
        o_b = diff_prompt(qb, kb, vb, tab_b, lam, subln_w[l], lambda_init)
        hp = merge_and_ffn(hp, xn, o_a, o_b, w_merge_gate[l], b_merge_gate[l], w_branch_a[l],
                           w_branch_b[l], w_out[l], g_ffn[l], w_ffn_up[l], w_ffn_down[l])
        ka_p.append(ka)
        va_p.append(va)
        kb_p.append(kb.reshape(kb.shape[:3] + (2 * HEAD_DIM,)))
        vb_p.append(vb)
        xn_s = rms_norm(hs, g_mix[l])
        qa_s, ka_s_, va_s_, qb_s, kb_s_, vb_s_ = split_projection(xn_s @ w_in[l])
        o_a_s = moba_sample(qa_s, ka_s_, va_s_, cache_k_a[l], cache_v_a[l], page_table, tab_a)
        o_b_s = diff_sample(qb_s, kb_s_, vb_s_, cache_k_b[l], cache_v_b[l], page_table, tab_b,
                            lam, subln_w[l], lambda_init)
        hs = merge_and_ffn(hs, xn_s, o_a_s, o_b_s, w_merge_gate[l], b_merge_gate[l], w_branch_a[l],
                           w_branch_b[l], w_out[l], g_ffn[l], w_ffn_up[l], w_ffn_down[l])
        ka_s.append(ka_s_)
        va_s.append(va_s_)
        kb_s.append(kb_s_.reshape(kb_s_.shape[:3] + (2 * HEAD_DIM,)))
        vb_s.append(vb_s_)
    y_prompt = rms_norm(hp, g_final)
    y_sample = rms_norm(hs, g_final)
    return (y_prompt, y_sample, jnp.stack(ka_p), jnp.stack(va_p), jnp.stack(kb_p), jnp.stack(vb_p),
            jnp.stack(ka_s), jnp.stack(va_s), jnp.stack(kb_s), jnp.stack(vb_s))
```

```python
import functools
import math

import numpy as np
import jax
import jax.numpy as jnp
from jax import lax
from jax.experimental import pallas as pl
from jax.experimental.pallas import tpu as pltpu

HEAD_DIM = 64
MOBA_HEADS = 8
DIFF_HEADS = 4
MOBA_BLOCK = 256
MOBA_TOPK = 3
REL_BUCKETS = 32
REL_MAX_DIST = 128
RMS_EPS = 1e-6
LAMBDA_INIT = 0.8 - 0.6 * math.exp(-0.3 * 0)

LANES = 128
ATT_WIDTH = 512
UNITS = ATT_WIDTH // LANES
TQ = MOBA_BLOCK
PAGES_PER_STEP = 8
VMEM_LIMIT = 56 * 1024 * 1024

F32 = jnp.float32
BF16 = jnp.bfloat16
NEG_INF = float("-inf")


def _rel_bucket_np(dist):
    n = np.maximum(dist, 0)
    exact = REL_BUCKETS // 2
    nf = np.maximum(n, 1).astype(np.float64)
    large = exact + (np.log(nf / exact) / math.log(REL_MAX_DIST / exact)
                     * (REL_BUCKETS - exact)).astype(np.int64)
    return np.where(n < exact, n, np.minimum(large, REL_BUCKETS - 1)).astype(np.int32)


def _bias_from_dist(tab, dist):
    b = tab[:, _rel_bucket_np(dist)]
    return jnp.where(jnp.asarray(dist >= 0), b, NEG_INF)


def _nt(a, b, **kw):
    return lax.dot_general(a, b, (((1,), (1,)), ((), ())), preferred_element_type=F32, **kw)


def _const_spec(shape):
    n = len(shape)
    return pl.BlockSpec(shape, lambda *_: (0,) * n, pipeline_mode=pl.Buffered(1))


def _proj_kernel(x_ref, g_ref, w_ref, qkv_ref, ka_ref, va_ref, kb_ref, vb_ref, kmean_ref):
    x = x_ref[...]
    xn = x * lax.rsqrt(jnp.mean(x * x, axis=-1, keepdims=True) + RMS_EPS) * g_ref[...]
    p = jnp.dot(xn.astype(BF16), w_ref[...], preferred_element_type=F32)
    w = ATT_WIDTH
    scale = HEAD_DIM ** -0.5
    ka = p[:, w:2 * w]
    ka_ref[...] = ka
    va_ref[...] = p[:, 2 * w:3 * w]
    kb_ref[...] = p[:, 4 * w:5 * w]
    vb_ref[...] = p[:, 5 * w:6 * w]
    qkv_ref[:, 0:w] = (p[:, 0:w] * scale).astype(BF16)
    qkv_ref[:, w:3 * w] = p[:, w:3 * w].astype(BF16)
    qkv_ref[:, 3 * w:4 * w] = (p[:, 3 * w:4 * w] * scale).astype(BF16)
    qkv_ref[:, 4 * w:6 * w] = p[:, 4 * w:6 * w].astype(BF16)
    kmean_ref[0] = jnp.mean(ka, axis=0, keepdims=True)


def _project(x2d, g, w_bf16, tm):
    n, d = x2d.shape
    w = ATT_WIDTH
    steps = n // tm
    row = lambda i: (i, 0)
    f32_out = jax.ShapeDtypeStruct((n, w), F32)
    return pl.pallas_call(
        _proj_kernel,
        grid=(steps,),
        in_specs=[pl.BlockSpec((tm, d), row), _const_spec((1, d)), _const_spec((d, 6 * w))],
        out_specs=[pl.BlockSpec((tm, 6 * w), row)] + [pl.BlockSpec((tm, w), row)] * 4
                  + [pl.BlockSpec((1, 1, w), lambda i: (i, 0, 0))],
        out_shape=[jax.ShapeDtypeStruct((n, 6 * w), BF16), f32_out, f32_out, f32_out, f32_out,
                   jax.ShapeDtypeStruct((steps, 1, w), F32)],
        compiler_params=pltpu.CompilerParams(dimension_semantics=("arbitrary",),
                                             vmem_limit_bytes=VMEM_LIMIT),
    )(x2d, g.reshape(1, d), w_bf16)


def _split_halves(q2):
    lane = lax.broadcasted_iota(jnp.int32, q2.shape, 1)
    zero = jnp.zeros_like(q2)
    return jnp.concatenate([jnp.where(lane < HEAD_DIM, q2, zero),
                            jnp.where(lane >= HEAD_DIM, q2, zero)], axis=0)


def _online_update(u, s, shift, v, m_ref, l_ref, acc_ref):
    m_old = m_ref[u]
    row_max = jnp.max(s, axis=-1, keepdims=True)
    if shift is None:
        m_new = jnp.maximum(m_old, row_max)
        p = jnp.exp(s - m_new)
    else:
        m_new = jnp.maximum(m_old, row_max + shift)
        p = jnp.exp(s - (m_new - shift))
    alpha = jnp.exp(m_old - m_new)
    l_ref[u] = alpha * l_ref[u] + jnp.sum(p, axis=-1, keepdims=True)
    acc_ref[u] = alpha * acc_ref[u] + jnp.dot(p.astype(BF16), v, preferred_element_type=F32)
    m_ref[u] = m_new


def _attn_kernel(moba, *refs):
    if moba:
        (q_ref, k_ref, v_ref, bias_ref, cfar_ref, kmean_ref, o_ref,
         qs_ref, m_ref, l_ref, acc_ref, sh_ref) = refs
    else:
        (q_ref, k_ref, v_ref, bias_ref, cfar_ref, lam_ref, subln_ref, o_ref,
         qs_ref, m_ref, l_ref, acc_ref) = refs
    i = pl.program_id(1)
    tq = TQ

    def kv_tile(ref, n, u):
        return ref[pl.ds(pl.multiple_of(n * tq, tq), tq), u * LANES:(u + 1) * LANES]

    for u in range(UNITS):
        qs = _split_halves(q_ref[:, u * LANES:(u + 1) * LANES])
        qs_ref[u] = qs
        s = _nt(qs, kv_tile(k_ref, i, u)) + bias_ref[u, 0]
        m = jnp.max(s, axis=-1, keepdims=True)
        p = jnp.exp(s - m)
        m_ref[u] = m
        l_ref[u] = jnp.sum(p, axis=-1, keepdims=True)
        acc_ref[u] = jnp.dot(p.astype(BF16), kv_tile(v_ref, i, u), preferred_element_type=F32)
        if moba:
            km = kmean_ref[0, :, u * LANES:(u + 1) * LANES]
            hi = km.astype(BF16)
            r1 = km - hi.astype(F32)
            mid = r1.astype(BF16)
            lo = (r1 - mid.astype(F32)).astype(BF16)
            g = _nt(qs, hi) + _nt(qs, mid) + _nt(qs, lo)
            nb = g.shape[1]
            col = lax.broadcasted_iota(jnp.int32, g.shape, 1)
            valid = col < i
            g = jnp.where(valid, g, NEG_INF)
            rank = jnp.zeros(g.shape, jnp.int32)
            for mm in range(nb):
                gm = g[:, mm:mm + 1]
                ahead = (gm > g) | ((gm == g) & (mm < col))
                rank = rank + ahead.astype(jnp.int32)
            sel = valid & (rank < MOBA_TOPK)
            sh_ref[u] = jnp.where(sel, cfar_ref[u], NEG_INF)

    def block_shift(u, n):
        shv = sh_ref[u]
        col = lax.broadcasted_iota(jnp.int32, shv.shape, 1)
        return jnp.max(jnp.where(col == n, shv, NEG_INF), axis=-1, keepdims=True)

    @pl.when(i >= 1)
    def _():
        for u in range(UNITS):
            s = _nt(qs_ref[u], kv_tile(k_ref, i - 1, u)) + bias_ref[u, 1]
            if moba:
                s = s + (block_shift(u, i - 1) - cfar_ref[u])
            _online_update(u, s, None, kv_tile(v_ref, i - 1, u), m_ref, l_ref, acc_ref)

    def far(n, carry):
        for u in range(UNITS):
            s = _nt(qs_ref[u], kv_tile(k_ref, n, u))
            shift = block_shift(u, n) if moba else cfar_ref[u]
            _online_update(u, s, shift, kv_tile(v_ref, n, u), m_ref, l_ref, acc_ref)
        return carry

    lax.fori_loop(0, i - 1, far, 0)

    lane = lax.broadcasted_iota(jnp.int32, (tq, LANES), 1)
    if not moba:
        lp = lam_ref[...]
        lam = (jnp.exp(jnp.sum(lp[0:1] * lp[1:2], axis=-1, keepdims=True))
               - jnp.exp(jnp.sum(lp[2:3] * lp[3:4], axis=-1, keepdims=True)) + LAMBDA_INIT)
    for u in range(UNITS):
        o = acc_ref[u] / l_ref[u]
        if moba:
            out = jnp.where(lane < HEAD_DIM, o[:tq], o[tq:])
        else:
            out = o[:tq] - lam * o[tq:]
            out = out * lax.rsqrt(jnp.mean(out * out, axis=-1, keepdims=True) + RMS_EPS)
            out = out * subln_ref[...] * (1.0 - LAMBDA_INIT)
        o_ref[:, u * LANES:(u + 1) * LANES] = out.astype(o_ref.dtype)


def _prompt_attention(moba, qkv, col0, batch, seq, bias, cfar, extra):
    nt = seq // TQ
    w = ATT_WIDTH
    in_specs = [
        pl.BlockSpec((TQ, w), lambda b, i: (b * nt + i, col0)),
        pl.BlockSpec((seq, w), lambda b, i: (b, col0 + 1)),
        pl.BlockSpec((seq, w), lambda b, i: (b, col0 + 2)),
        _const_spec(bias.shape),
        _const_spec(cfar.shape),
    ]
    scratch = [pltpu.VMEM((UNITS, 2 * TQ, LANES), BF16),
               pltpu.VMEM((UNITS, 2 * TQ, 1), F32),
               pltpu.VMEM((UNITS, 2 * TQ, 1), F32),
               pltpu.VMEM((UNITS, 2 * TQ, LANES), F32)]
    if moba:
        (kmean,) = extra
        in_specs.append(pl.BlockSpec((1, nt, w), lambda b, i: (b, 0, 0)))
        scratch.append(pltpu.VMEM((UNITS, 2 * TQ, nt), F32))
    else:
        lam_params, subln = extra
        in_specs += [_const_spec(lam_params.shape), _const_spec(subln.shape)]
    return pl.pallas_call(
        functools.partial(_attn_kernel, moba),
        grid=(batch, nt),
        in_specs=in_specs,
        out_specs=pl.BlockSpec((TQ, w), lambda b, i: (b * nt + i, 0)),
        out_shape=jax.ShapeDtypeStruct((batch * seq, w), BF16),
        scratch_shapes=scratch,
        compiler_params=pltpu.CompilerParams(dimension_semantics=("arbitrary", "arbitrary"),
                                             vmem_limit_bytes=VMEM_LIMIT),
    )(qkv, qkv, qkv, bias, cfar, *extra)


def _prompt_bias(tab, heads_of_unit):
    r = np.arange(TQ)[:, None] - np.arange(TQ)[None, :]
    own = _bias_from_dist(tab, r)
    prev = _bias_from_dist(tab, r + TQ)
    far = tab[:, REL_BUCKETS - 1]
    bias, cfar = [], []
    for h0, h1 in heads_of_unit:
        bias.append(jnp.stack([jnp.concatenate([own[h0], own[h1]], axis=0),
                               jnp.concatenate([prev[h0], prev[h1]], axis=0)]))
        cfar.append(jnp.concatenate([jnp.full((TQ, 1), far[h0]), jnp.full((TQ, 1), far[h1])]))
    return jnp.stack(bias), jnp.stack(cfar)


def _merge_ffn_kernel(ffn_chunk, x_ref, oa_ref, ob_ref, gmix_ref, wg_ref, bg_ref, wa_ref, wb_ref,
                      wo_ref, gffn_ref, wup_ref, wdn_ref, gfin_ref, y_ref):
    x = x_ref[...]
    d = x.shape[1]
    xn = (x * lax.rsqrt(jnp.mean(x * x, axis=-1, keepdims=True) + RMS_EPS) * gmix_ref[...]).astype(BF16)
    gates = jax.nn.sigmoid(jnp.dot(xn, wg_ref[...], preferred_element_type=F32) + bg_ref[...])
    merged = (gates[:, :d] * jnp.dot(oa_ref[...], wa_ref[...], preferred_element_type=F32)
              + gates[:, d:] * jnp.dot(ob_ref[...], wb_ref[...], preferred_element_type=F32))
    h = x + jnp.dot(merged.astype(BF16), wo_ref[...], preferred_element_type=F32)
    hn = (h * lax.rsqrt(jnp.mean(h * h, axis=-1, keepdims=True) + RMS_EPS) * gffn_ref[...]).astype(BF16)
    hidden = wdn_ref.shape[0]
    out = h
    for c in range(hidden // ffn_chunk):
        lo = c * ffn_chunk
        gate = jnp.dot(hn, wup_ref[:, lo:lo + ffn_chunk], preferred_element_type=F32)
        up = jnp.dot(hn, wup_ref[:, hidden + lo:hidden + lo + ffn_chunk], preferred_element_type=F32)
        act = (gate * jax.nn.sigmoid(gate) * up).astype(BF16)
        out = out + jnp.dot(act, wdn_ref[lo:lo + ffn_chunk, :], preferred_element_type=F32)
    y = out * lax.rsqrt(jnp.mean(out * out, axis=-1, keepdims=True) + RMS_EPS) * gfin_ref[...]
    y_ref[...] = y


def _merge_ffn(x2d, oa, ob, weights, tm):
    n, d = x2d.shape
    row = lambda i: (i, 0)
    in_specs = [pl.BlockSpec((tm, d), row), pl.BlockSpec((tm, ATT_WIDTH), row),
                pl.BlockSpec((tm, ATT_WIDTH), row)] + [_const_spec(w.shape) for w in weights]
    hidden = weights[-2].shape[0]
    return pl.pallas_call(
        functools.partial(_merge_ffn_kernel, math.gcd(hidden, 2 * LANES)),
        grid=(n // tm,),
        in_specs=in_specs,
        out_specs=pl.BlockSpec((tm, d), row),
        out_shape=jax.ShapeDtypeStruct((n, d), F32),
        compiler_params=pltpu.CompilerParams(dimension_semantics=("arbitrary",),
                                             vmem_limit_bytes=VMEM_LIMIT),
    )(x2d, oa, ob, *weights)


def _top_lanes(g, count):
    lane = lax.broadcasted_iota(jnp.int32, g.shape, 1)
    out = jnp.zeros(g.shape, jnp.int32)
    for k in range(count):
        best = jnp.max(g, axis=-1, keepdims=True)
        idx = jnp.min(jnp.where(g == best, lane, LANES), axis=-1, keepdims=True)
        out = jnp.where(lane == k, idx, out)
        g = jnp.where(lane == idx, NEG_INF, g)
    return out


def _stream_kernel(ppb, pt_ref, qb_ref, qa_ref, knew_ref, vnew_ref, bnew_ref, bias_ref, lam_ref,
                   subln_ref, *refs):
    del pt_ref
    pps = PAGES_PER_STEP
    kb_refs, vb_refs, ka_refs = refs[:pps], refs[pps:2 * pps], refs[2 * pps:3 * pps]
    ob_ref, idx_ref, m_ref, l_ref, acc_ref, gate_ref = refs[3 * pps:]
    c = pl.program_id(1)
    nc = pl.num_programs(1)
    qb = qb_ref[0]
    half = qb.shape[0] // 2

    @pl.when(c == 0)
    def _():
        s = _nt(qb, knew_ref[0]) + bnew_ref[...]
        m = jnp.max(s, axis=-1, keepdims=True)
        p = jnp.exp(s - m)
        m_ref[...] = m
        l_ref[...] = jnp.sum(p, axis=-1, keepdims=True)
        acc_ref[...] = jnp.dot(p, vnew_ref[0], preferred_element_type=F32)
        gate_ref[...] = jnp.full(gate_ref.shape, NEG_INF, F32)

    s = jnp.concatenate([_nt(qb, kr[...]) for kr in kb_refs], axis=-1) + bias_ref[0]
    m_old = m_ref[...]
    m_new = jnp.maximum(m_old, jnp.max(s, axis=-1, keepdims=True))
    p = jnp.exp(s - m_new)
    alpha = jnp.exp(m_old - m_new)
    l_ref[...] = alpha * l_ref[...] + jnp.sum(p, axis=-1, keepdims=True)
    width = kb_refs[0].shape[0]
    pv = jnp.dot(p[:, :width], vb_refs[0][...], preferred_element_type=F32)
    for j in range(1, pps):
        pv = pv + jnp.dot(p[:, j * width:(j + 1) * width], vb_refs[j][...], preferred_element_type=F32)
    acc_ref[...] = alpha * acc_ref[...] + pv
    m_ref[...] = m_new

    qa = qa_ref[0]
    gate = gate_ref[...]
    lane = lax.broadcasted_iota(jnp.int32, gate.shape, 1)
    for j in range(pps // ppb):
        sa = jnp.dot(qa, ka_refs[j * ppb][...], preferred_element_type=F32)
        for jj in range(1, ppb):
            sa = sa + jnp.dot(qa, ka_refs[j * ppb + jj][...], preferred_element_type=F32)
        g = jnp.sum(sa, axis=-1, keepdims=True)
        gate = jnp.where(lane == c * (pps // ppb) + j, g, gate)
    gate_ref[...] = gate

    @pl.when(c == nc - 1)
    def _():
        lp = lam_ref[...]
        lam = (jnp.exp(jnp.sum(lp[0:1] * lp[1:2], axis=-1, keepdims=True))
               - jnp.exp(jnp.sum(lp[2:3] * lp[3:4], axis=-1, keepdims=True)) + LAMBDA_INIT)
        o = acc_ref[...] / l_ref[...]
        out = o[:half] - lam * o[half:]
        out = out * lax.rsqrt(jnp.mean(out * out, axis=-1, keepdims=True) + RMS_EPS)
        ob_ref[0] = out * subln_ref[...] * (1.0 - LAMBDA_INIT)
        idx_ref[0] = _top_lanes(gate, MOBA_TOPK)


def _sample_stream(page_table, qb_rows, qa_rows, knew, vnew, bnew, bias, lam_params, subln,
                   pool_kb, pool_vb, pool_ka_t, page):
    db, n_pages = page_table.shape
    pps = PAGES_PER_STEP
    nc = n_pages // pps
    rows_b, rows_a = qb_rows.shape[1], qa_rows.shape[1]
    blk_b, blk_a = pool_kb.shape[1:], pool_ka_t.shape[1:]

    def page_spec(j, blk):
        return pl.BlockSpec((None,) + blk, lambda b, c, pt: (pt[b * n_pages + c * pps + j], 0, 0))

    per_b = lambda b, c, pt: (b, 0, 0)
    fixed = lambda b, c, pt: (0, 0)
    in_specs = [
        pl.BlockSpec((1,) + qb_rows.shape[1:], per_b),
        pl.BlockSpec((1,) + qa_rows.shape[1:], per_b),
        pl.BlockSpec((1,) + knew.shape[1:], per_b),
        pl.BlockSpec((1,) + vnew.shape[1:], per_b),
        pl.BlockSpec(bnew.shape, fixed),
        pl.BlockSpec((1,) + bias.shape[1:], lambda b, c, pt: (jnp.where(c == nc - 1, 1, 0), 0, 0)),
        pl.BlockSpec(lam_params.shape, fixed),
        pl.BlockSpec(subln.shape, fixed),
    ] + [page_spec(j, blk_b) for j in range(pps)] * 2 + [page_spec(j, blk_a) for j in range(pps)]
    grid_spec = pltpu.PrefetchScalarGridSpec(
        num_scalar_prefetch=1,
        grid=(db, nc),
        in_specs=in_specs,
        out_specs=[pl.BlockSpec((1, rows_b // 2, LANES), per_b), pl.BlockSpec((1, rows_a, LANES), per_b)],
        scratch_shapes=[pltpu.VMEM((rows_b, 1), F32), pltpu.VMEM((rows_b, 1), F32),
                        pltpu.VMEM((rows_b, LANES), F32), pltpu.VMEM((rows_a, LANES), F32)],
    )
    return pl.pallas_call(
        functools.partial(_stream_kernel, MOBA_BLOCK // page),
        grid_spec=grid_spec,
        out_shape=[jax.ShapeDtypeStruct((db, rows_b // 2, LANES), F32),
                   jax.ShapeDtypeStruct((db, rows_a, LANES), jnp.int32)],
        compiler_params=pltpu.CompilerParams(dimension_semantics=("arbitrary", "arbitrary"),
                                             vmem_limit_bytes=VMEM_LIMIT),
    )(page_table.reshape(-1), qb_rows, qa_rows, knew, vnew, bnew, bias, lam_params, subln,
      *([pool_kb] * pps), *([pool_vb] * pps), *([pool_ka_t] * pps))


def _moba_sample_kernel(ppb, last_block, pt_ref, sel_ref, q_ref, knew_ref, vnew_ref, bown_ref,
                        bnear_ref, cfar_ref, *refs):
    del pt_ref
    n_sel = ppb * MOBA_TOPK
    k_refs, v_refs = refs[:n_sel], refs[n_sel:2 * n_sel]
    o_ref = refs[2 * n_sel]
    b, h, t = pl.program_id(0), pl.program_id(1), pl.program_id(2)
    n_heads, t_new = pl.num_programs(1), pl.num_programs(2)

    @pl.when(t == 0)
    def _():
        o_ref[...] = jnp.zeros_like(o_ref)

    q1 = q_ref[0, 0, pl.ds(t, 1), :]
    q8 = jnp.broadcast_to(q1, (8, HEAD_DIM))
    s_own = jnp.sum(q1 * knew_ref[0, 0], axis=-1, keepdims=True) + bown_ref[0, 0]
    scores = []
    for j in range(n_sel):
        bias = cfar_ref[0]
        if j % ppb == ppb - 1:
            blk = sel_ref[((b * n_heads + h) * t_new + t) * MOBA_TOPK + j // ppb]
            bias = jnp.where(blk == last_block, bnear_ref[0, pl.ds(t, 1), :], bias)
        scores.append(jnp.dot(q8, k_refs[j][...], preferred_element_type=F32) + bias)
    s = jnp.concatenate(scores, axis=-1)
    m = jnp.maximum(jnp.max(s, axis=-1, keepdims=True)[0:1], jnp.max(s_own, axis=0, keepdims=True))
    p = jnp.exp(s - m)
    p_own = jnp.exp(s_own - m)
    l = jnp.sum(p, axis=-1, keepdims=True)[0:1] + jnp.sum(p_own, axis=0, keepdims=True)
    acc = jnp.sum(p_own * vnew_ref[0, 0], axis=0, keepdims=True)
    page = k_refs[0].shape[1]
    for j in range(n_sel):
        acc = acc + _nt(p[:, j * page:(j + 1) * page], v_refs[j][...])[0:1]
    o_ref[0, 0, pl.ds(t, 1), :] = acc / l


def _moba_sample(page_table, sel, q8, knew8, vnew8, bown, bnear, cfar, pool_k_t, pool_v_t):
    db, n_pages = page_table.shape
    page = pool_k_t.shape[3]
    t_new = sel.shape[2]
    ppb = MOBA_BLOCK // page
    last_block = n_pages // ppb - 1

    def page_spec(j):
        k, p = j // ppb, j % ppb

        def index(b, h, t, pt, sl):
            blk = sl[((b * MOBA_HEADS + h) * t_new + t) * MOBA_TOPK + k]
            return (pt[b * n_pages + blk * ppb + p], h, 0, 0)
        return pl.BlockSpec((None, None, HEAD_DIM, page), index)

    n_sel = MOBA_TOPK * ppb
    per_bh = lambda b, h, t, pt, sl: (b, h, 0, 0)
    in_specs = [
        pl.BlockSpec((1, 1, 8, HEAD_DIM), per_bh),
        pl.BlockSpec((1, 1, 8, HEAD_DIM), per_bh),
        pl.BlockSpec((1, 1, 8, HEAD_DIM), per_bh),
        pl.BlockSpec((1, 1, 8, 1), lambda b, h, t, pt, sl: (h, t, 0, 0)),
        pl.BlockSpec((1, 8, page), lambda b, h, t, pt, sl: (h, 0, 0)),
        pl.BlockSpec((1, 1, page), lambda b, h, t, pt, sl: (h, 0, 0)),
    ] + [page_spec(j) for j in range(n_sel)] * 2
    grid_spec = pltpu.PrefetchScalarGridSpec(
        num_scalar_prefetch=2,
        grid=(db, MOBA_HEADS, t_new),
        in_specs=in_specs,
        out_specs=pl.BlockSpec((1, 1, 8, HEAD_DIM), per_bh),
    )
    return pl.pallas_call(
        functools.partial(_moba_sample_kernel, ppb, last_block),
        grid_spec=grid_spec,
        out_shape=jax.ShapeDtypeStruct((db, MOBA_HEADS, 8, HEAD_DIM), F32),
        compiler_params=pltpu.CompilerParams(
            dimension_semantics=("arbitrary", "arbitrary", "arbitrary")),
    )(page_table.reshape(-1), sel.reshape(-1), q8, knew8, vnew8, bown, bnear, cfar,
      *([pool_k_t] * n_sel), *([pool_v_t] * n_sel))


def kernel(x_prompt, x_sample, cache_k_a, cache_v_a, cache_k_b, cache_v_b, page_table, g_mix, w_in,
           w_merge_gate, b_merge_gate, w_branch_a, w_branch_b, w_out, lambda_q1, lambda_k1, lambda_q2,
           lambda_k2, subln_w, g_ffn, w_ffn_up, w_ffn_down, rel_bias, g_final):
    batch, seq, d = x_prompt.shape
    db, t_new, _ = x_sample.shape
    depth, n_pool, page = cache_k_a.shape[:3]
    n_pages = page_table.shape[1]
    past = n_pages * page
    nb_past = past // MOBA_BLOCK
    w = ATT_WIDTH
    assert depth == 1 and seq % TQ == 0 and past % MOBA_BLOCK == 0 and MOBA_BLOCK % page == 0
    assert n_pages % PAGES_PER_STEP == 0 and MOBA_TOPK <= nb_past <= LANES and t_new <= 8
    assert page == LANES and PAGES_PER_STEP % (MOBA_BLOCK // page) == 0

    tab = rel_bias.astype(F32).T
    tab_a, tab_b = tab[:MOBA_HEADS], tab[MOBA_HEADS:]
    far_a, far_b = tab_a[:, REL_BUCKETS - 1], tab_b[:, REL_BUCKETS - 1]
    lam_params = jnp.concatenate([lambda_q1, lambda_k1, lambda_q2, lambda_k2], axis=0).astype(F32)
    subln = subln_w.astype(F32).reshape(1, 2 * HEAD_DIM)
    w_in_b = w_in[0].astype(BF16)
    merge_weights = (g_mix.reshape(1, d), w_merge_gate[0].astype(BF16), b_merge_gate.reshape(1, 2 * d),
                     w_branch_a[0].astype(BF16), w_branch_b[0].astype(BF16), w_out[0].astype(BF16),
                     g_ffn.reshape(1, d), w_ffn_up[0].astype(BF16), w_ffn_down[0].astype(BF16),
                     g_final.reshape(1, d))

    xp = x_prompt.reshape(batch * seq, d)
    qkv, ka, va, kb, vb, kmean = _project(xp, g_mix, w_in_b, TQ)
    bias_a, cfar_a = _prompt_bias(tab_a, [(2 * u, 2 * u + 1) for u in range(UNITS)])
    bias_b, cfar_b = _prompt_bias(tab_b, [(u, u) for u in range(UNITS)])
    o_a = _prompt_attention(True, qkv, 0, batch, seq, bias_a, cfar_a,
                            (kmean.reshape(batch, seq // TQ, w),))
    o_b = _prompt_attention(False, qkv, 3, batch, seq, bias_b, cfar_b, (lam_params, subln))
    y_prompt = _merge_ffn(xp, o_a, o_b, merge_weights, TQ).reshape(batch, seq, d)

    n_s = db * t_new
    xs = x_sample.reshape(n_s, d)
    qkv_s, ka_s, va_s, kb_s, vb_s, _ = _project(xs, g_mix, w_in_b, n_s)
    qkv_s3 = qkv_s.reshape(db, t_new, 6 * w).astype(F32)
    pad_to = lambda a, axis, n: jnp.pad(a, [(0, n - a.shape[i] if i == axis else 0) for i in range(a.ndim)])
    hb = DIFF_HEADS
    pool_kb = cache_k_b.reshape(n_pool, page * hb, 2 * HEAD_DIM)
    pool_vb = cache_v_b.reshape(n_pool, page * hb, 2 * HEAD_DIM)
    pool_ka_t = jnp.transpose(cache_k_a[0], (0, 2, 3, 1))
    pool_va_t = jnp.transpose(cache_v_a[0], (0, 2, 3, 1))

    lane = np.arange(LANES)
    qb16 = qkv_s3[:, :, 3 * w:4 * w].reshape(db, t_new, hb, LANES).transpose(0, 2, 1, 3)
    qb16 = qb16.reshape(db, hb * t_new, LANES)
    qb_rows = jnp.concatenate([jnp.where(lane < HEAD_DIM, qb16, 0.0),
                               jnp.where(lane >= HEAD_DIM, qb16, 0.0)], axis=1)
    rows_b = 2 * hb * t_new
    head_of_row = (np.arange(rows_b) // t_new) % hb
    tok_of_row = np.arange(rows_b) % t_new
    head_mask = lambda cols: jnp.asarray(head_of_row[:, None] == (cols % hb)[None, :])
    pick = lambda a: a[head_of_row, np.arange(rows_b)]
    cols_new = np.arange(LANES)
    d_new = np.where(cols_new[None, :] < t_new * hb, tok_of_row[:, None] - (cols_new // hb)[None, :], -1)
    bnew = jnp.where(head_mask(cols_new), pick(_bias_from_dist(tab_b, d_new)), NEG_INF)
    span = PAGES_PER_STEP * page
    cols = np.arange(span * hb)
    d_last = (past + tok_of_row)[:, None] - (past - span + cols // hb)[None, :]
    bias_s = jnp.stack([jnp.broadcast_to(far_b[head_of_row][:, None], (rows_b, span * hb)),
                        pick(_bias_from_dist(tab_b, d_last))])
    bias_s = jnp.where(head_mask(cols)[None], bias_s, NEG_INF)
    seg = np.arange(w) // HEAD_DIM
    qa_s = qkv_s3[:, :, 0:w]
    qa_rows = jnp.where(jnp.asarray(seg[None, None, :] == np.arange(MOBA_HEADS)[:, None, None]),
                        qa_s[:, None], 0.0).reshape(db, MOBA_HEADS * t_new, w)
    knew_b = pad_to(kb_s.reshape(db, t_new * hb, 2 * HEAD_DIM), 1, LANES)
    vnew_b = pad_to(vb_s.reshape(db, t_new * hb, 2 * HEAD_DIM), 1, LANES)
    o_b16, sel_rows = _sample_stream(
        page_table, qb_rows, qa_rows, knew_b, vnew_b, bnew, bias_s, lam_params, subln,
        pool_kb, pool_vb, pool_ka_t.reshape(n_pool, w, page), page)
    o_b_s = o_b16.reshape(db, hb, t_new, LANES).transpose(0, 2, 1, 3).reshape(n_s, w).astype(BF16)
    sel = sel_rows[:, :, :MOBA_TOPK].reshape(db, MOBA_HEADS, t_new, MOBA_TOPK)

    heads_first = lambda a: pad_to(a.reshape(db, t_new, MOBA_HEADS, HEAD_DIM).transpose(0, 2, 1, 3), 2, 8)
    d_own = np.arange(t_new)[:, None] - np.arange(8)[None, :]
    d_own = np.where(np.arange(8)[None, :] < t_new, d_own, -1)
    bown = _bias_from_dist(tab_a, d_own)[..., None]
    d_near = (past + np.arange(8))[:, None] - (past - page + np.arange(page))[None, :]
    bnear = _bias_from_dist(tab_a, d_near)
    cfar_s = jnp.broadcast_to(far_a[:, None, None], (MOBA_HEADS, 1, page))
    o_heads = _moba_sample(page_table, sel, heads_first(qa_s), heads_first(ka_s), heads_first(va_s),
                           bown, bnear, cfar_s, pool_ka_t, pool_va_t)
    o_a_s = o_heads[:, :, :t_new].transpose(0, 2, 1, 3).reshape(n_s, w).astype(BF16)
    y_sample = _merge_ffn(xs, o_a_s, o_b_s, merge_weights, n_s).reshape(db, t_new, d)

    a_shape = (MOBA_HEADS, HEAD_DIM)
    b_shape = (DIFF_HEADS, 2 * HEAD_DIM)
    return (y_prompt, y_sample,
            ka.reshape((1, batch, seq) + a_shape), va.reshape((1, batch, seq) + a_shape),
            kb.reshape((1, batch, seq) + b_shape), vb.reshape((1, batch, seq) + b_shape),
            ka_s.reshape((1, db, t_new) + a_shape), va_s.reshape((1, db, t_new) + a_shape),
            kb_s.reshape((1, db, t_new) + b_shape), vb_s.reshape((1, db, t_new) + b_shape))
```

```python
import functools
import math

import numpy as np
import jax
import jax.numpy as jnp
from jax import lax
from jax.experimental import pallas as pl
from jax.experimental.pallas import tpu as pltpu

HEAD_DIM = 64
MOBA_HEADS = 8
DIFF_HEADS = 4
MOBA_BLOCK = 256
MOBA_TOPK = 3
REL_BUCKETS = 32
REL_MAX_DIST = 128
RMS_EPS = 1e-6
LAMBDA_INIT = 0.8 - 0.6 * math.exp(-0.3 * 0)

LANES = 128
ATT_WIDTH = 512
UNITS = ATT_WIDTH // LANES
TQ = MOBA_BLOCK
PAGES_PER_STEP = 8
VMEM_LIMIT = 56 * 1024 * 1024

F32 = jnp.float32
BF16 = jnp.bfloat16
NEG_INF = float("-inf")


def _rel_bucket_np(dist):
    n = np.maximum(dist, 0)
    exact = REL_BUCKETS // 2
    nf = np.maximum(n, 1).astype(np.float64)
    large = exact + (np.log(nf / exact) / math.log(REL_MAX_DIST / exact)
                     * (REL_BUCKETS - exact)).astype(np.int64)
    return np.where(n < exact, n, np.minimum(large, REL_BUCKETS - 1)).astype(np.int32)


def _bias_from_dist(tab, dist, per_row=False):
    idx = jnp.asarray(_rel_bucket_np(dist))
    col = (lambda k: tab[:, k:k + 1]) if per_row else (
        lambda k: tab[:, k].reshape((-1,) + (1,) * dist.ndim))
    b = jnp.broadcast_to(col(0), dist.shape if per_row else (tab.shape[0],) + dist.shape)
    for k in range(1, REL_BUCKETS):
        b = jnp.where(idx == k, col(k), b)
    return jnp.where(jnp.asarray(dist >= 0), b, NEG_INF)


def _nt(a, b, **kw):
    return lax.dot_general(a, b, (((1,), (1,)), ((), ())), preferred_element_type=F32, **kw)


def _const_spec(shape):
    n = len(shape)
    return pl.BlockSpec(shape, lambda *_: (0,) * n, pipeline_mode=pl.Buffered(1))


def _proj_kernel(x_ref, g_ref, w_ref, qk_ref, ka_ref, va_ref, kb_ref, vb_ref, vat_ref, vbt_ref,
                 kmean_ref):
    x = x_ref[...]
    xn = x * lax.rsqrt(jnp.mean(x * x, axis=-1, keepdims=True) + RMS_EPS) * g_ref[...]
    p = jnp.dot(xn.astype(BF16), w_ref[...], preferred_element_type=F32)
    w = ATT_WIDTH
    scale = HEAD_DIM ** -0.5
    ka, va = p[:, w:2 * w], p[:, 2 * w:3 * w]
    kb, vb = p[:, 4 * w:5 * w], p[:, 5 * w:6 * w]
    ka_ref[...] = ka
    va_ref[...] = va
    kb_ref[...] = kb
    vb_ref[...] = vb
    qk_ref[:, 0:w] = (p[:, 0:w] * scale).astype(BF16)
    qk_ref[:, w:2 * w] = ka.astype(BF16)
    qk_ref[:, 2 * w:3 * w] = (p[:, 3 * w:4 * w] * scale).astype(BF16)
    qk_ref[:, 3 * w:4 * w] = kb.astype(BF16)
    vat_ref[0] = va.T.astype(BF16)
    vbt_ref[0] = vb.T.astype(BF16)
    kmean_ref[0] = jnp.mean(ka, axis=0, keepdims=True)


def _project(x2d, g, w_bf16, tm):
    n, d = x2d.shape
    w = ATT_WIDTH
    steps = n // tm
    row = lambda i: (i, 0)
    f32_out = jax.ShapeDtypeStruct((n, w), F32)
    return pl.pallas_call(
        _proj_kernel,
        grid=(steps,),
        in_specs=[pl.BlockSpec((tm, d), row), _const_spec((1, d)), _const_spec((d, 6 * w))],
        out_specs=[pl.BlockSpec((tm, 4 * w), row)] + [pl.BlockSpec((tm, w), row)] * 4
                  + [pl.BlockSpec((1, w, tm), lambda i: (i, 0, 0))] * 2
                  + [pl.BlockSpec((1, 1, w), lambda i: (i, 0, 0))],
        out_shape=[jax.ShapeDtypeStruct((n, 4 * w), BF16), f32_out, f32_out, f32_out, f32_out,
                   jax.ShapeDtypeStruct((steps, w, tm), BF16), jax.ShapeDtypeStruct((steps, w, tm), BF16),
                   jax.ShapeDtypeStruct((steps, 1, w), F32)],
        compiler_params=pltpu.CompilerParams(dimension_semantics=("arbitrary",),
                                             vmem_limit_bytes=VMEM_LIMIT),
    )(x2d, g.reshape(1, d), w_bf16)


def _split_halves(q2):
    lane = lax.broadcasted_iota(jnp.int32, q2.shape, 1)
    zero = jnp.zeros_like(q2)
    return jnp.concatenate([jnp.where(lane < HEAD_DIM, q2, zero),
                            jnp.where(lane >= HEAD_DIM, q2, zero)], axis=0)


def _online_update(u, s, shift, vt, m_ref, l_ref, acc_ref):
    m_old = m_ref[u]
    m_new = jnp.maximum(m_old, jnp.max(s, axis=0, keepdims=True) + shift)
    p = jnp.exp(s - (m_new - shift))
    alpha = jnp.exp(m_old - m_new)
    l_ref[u] = alpha * l_ref[u] + jnp.sum(p, axis=0, keepdims=True)
    acc_ref[u] = alpha * acc_ref[u] + jnp.dot(vt, p.astype(BF16), preferred_element_type=F32)
    m_ref[u] = m_new


def _units_ahead(score_fn, fold_fn):
    s_next = score_fn(0)
    for u in range(UNITS):
        s = s_next
        if u + 1 < UNITS:
            s_next = score_fn(u + 1)
        fold_fn(u, s)


def _attn_kernel(moba, *refs):
    if moba:
        (q_ref, k_ref, vt_ref, bias_ref, cfar_ref, kmean_ref, o_ref,
         qs_ref, m_ref, l_ref, acc_ref, sh_ref) = refs
    else:
        (q_ref, k_ref, vt_ref, bias_ref, cfar_ref, lam_ref, subln_ref, o_ref,
         qs_ref, m_ref, l_ref, acc_ref) = refs
    i = pl.program_id(1)
    tq = TQ

    def k_tile(n, u):
        return k_ref[pl.ds(pl.multiple_of(n * tq, tq), tq), u * LANES:(u + 1) * LANES]

    def vt_tile(n, u):
        return vt_ref[n, u * LANES:(u + 1) * LANES, :]

    def scores(n):
        return lambda u: _nt(k_tile(n, u), qs_ref[u])

    for u in range(UNITS):
        qs = _split_halves(q_ref[:, u * LANES:(u + 1) * LANES])
        qs_ref[u] = qs
        if moba:
            km = kmean_ref[0, :, u * LANES:(u + 1) * LANES]
            hi = km.astype(BF16)
            r1 = km - hi.astype(F32)
            mid = r1.astype(BF16)
            lo = (r1 - mid.astype(F32)).astype(BF16)
            g = _nt(hi, qs) + _nt(mid, qs) + _nt(lo, qs)
            nb = g.shape[0]
            blk = lax.broadcasted_iota(jnp.int32, g.shape, 0)
            valid = blk < i
            g = jnp.where(valid, g, NEG_INF)
            rank = jnp.zeros(g.shape, jnp.int32)
            for mm in range(nb):
                gm = g[mm:mm + 1, :]
                ahead = (gm > g) | ((gm == g) & (mm < blk))
                rank = rank + ahead.astype(jnp.int32)
            sel = valid & (rank < MOBA_TOPK)
            sh_ref[u] = jnp.where(sel, cfar_ref[u], NEG_INF)

    def fold_own(u, s):
        s = s + bias_ref[u, 0]
        m = jnp.max(s, axis=0, keepdims=True)
        p = jnp.exp(s - m)
        m_ref[u] = m
        l_ref[u] = jnp.sum(p, axis=0, keepdims=True)
        acc_ref[u] = jnp.dot(vt_tile(i, u), p.astype(BF16), preferred_element_type=F32)

    _units_ahead(scores(i), fold_own)

    @pl.when(i >= 1)
    def _():
        def fold_prev(u, s):
            if moba:
                shift = sh_ref[u, pl.ds(i - 1, 1), :] - cfar_ref[u]
            else:
                shift = jnp.zeros((1, 2 * tq), F32)
            _online_update(u, s + bias_ref[u, 1], shift, vt_tile(i - 1, u), m_ref, l_ref, acc_ref)

        _units_ahead(scores(i - 1), fold_prev)

    def far(n, carry):
        def fold_far(u, s):
            shift = sh_ref[u, pl.ds(n, 1), :] if moba else cfar_ref[u]
            _online_update(u, s, shift, vt_tile(n, u), m_ref, l_ref, acc_ref)

        _units_ahead(scores(n), fold_far)
        return carry

    lax.fori_loop(0, i - 1, far, 0)

    chan = lax.broadcasted_iota(jnp.int32, (LANES, tq), 0)
    if not moba:
        lp = lam_ref[...]
        lam = (jnp.exp(jnp.sum(lp[0:1] * lp[1:2], axis=-1, keepdims=True))
               - jnp.exp(jnp.sum(lp[2:3] * lp[3:4], axis=-1, keepdims=True)) + LAMBDA_INIT)
    for u in range(UNITS):
        o = acc_ref[u] / l_ref[u]
        if moba:
            out = jnp.where(chan < HEAD_DIM, o[:, :tq], o[:, tq:])
        else:
            out = o[:, :tq] - lam * o[:, tq:]
            out = out * lax.rsqrt(jnp.mean(out * out, axis=0, keepdims=True) + RMS_EPS)
            out = out * subln_ref[...] * (1.0 - LAMBDA_INIT)
        o_ref[:, u * LANES:(u + 1) * LANES] = out.T.astype(o_ref.dtype)


def _prompt_attention(moba, qk, vt, col0, batch, seq, bias, cfar, extra):
    nt = seq // TQ
    w = ATT_WIDTH
    in_specs = [
        pl.BlockSpec((TQ, w), lambda b, i: (b * nt + i, col0)),
        pl.BlockSpec((seq, w), lambda b, i: (b, col0 + 1)),
        pl.BlockSpec((nt, w, TQ), lambda b, i: (b, 0, 0)),
        _const_spec(bias.shape),
        _const_spec(cfar.shape),
    ]
    scratch = [pltpu.VMEM((UNITS, 2 * TQ, LANES), BF16),
               pltpu.VMEM((UNITS, 1, 2 * TQ), F32),
               pltpu.VMEM((UNITS, 1, 2 * TQ), F32),
               pltpu.VMEM((UNITS, LANES, 2 * TQ), F32)]
    if moba:
        (kmean,) = extra
        in_specs.append(pl.BlockSpec((1, nt, w), lambda b, i: (b, 0, 0)))
        scratch.append(pltpu.VMEM((UNITS, nt, 2 * TQ), F32))
    else:
        lam_params, subln_col = extra
        in_specs += [_const_spec(lam_params.shape), _const_spec(subln_col.shape)]
    return pl.pallas_call(
        functools.partial(_attn_kernel, moba),
        grid=(batch, nt),
        in_specs=in_specs,
        out_specs=pl.BlockSpec((TQ, w), lambda b, i: (b * nt + i, 0)),
        out_shape=jax.ShapeDtypeStruct((batch * seq, w), BF16),
        scratch_shapes=scratch,
        compiler_params=pltpu.CompilerParams(dimension_semantics=("arbitrary", "arbitrary"),
                                             vmem_limit_bytes=VMEM_LIMIT),
    )(qk, qk, vt, bias, cfar, *extra)


def _prompt_bias(tab, heads_of_unit):
    r = np.arange(TQ)[None, :] - np.arange(TQ)[:, None]
    own = _bias_from_dist(tab, r)
    prev = _bias_from_dist(tab, r + TQ)
    far = tab[:, REL_BUCKETS - 1]
    bias, cfar = [], []
    for h0, h1 in heads_of_unit:
        bias.append(jnp.stack([jnp.concatenate([own[h0], own[h1]], axis=1),
                               jnp.concatenate([prev[h0], prev[h1]], axis=1)]))
        cfar.append(jnp.concatenate([jnp.full((1, TQ), far[h0]), jnp.full((1, TQ), far[h1])], axis=1))
    return jnp.stack(bias), jnp.stack(cfar)


def _merge_ffn_kernel(ffn_chunk, x_ref, oa_ref, ob_ref, gmix_ref, wg_ref, bg_ref, wa_ref, wb_ref,
                      wo_ref, gffn_ref, wup_ref, wdn_ref, gfin_ref, y_ref):
    x = x_ref[...]
    d = x.shape[1]
    xn = (x * lax.rsqrt(jnp.mean(x * x, axis=-1, keepdims=True) + RMS_EPS) * gmix_ref[...]).astype(BF16)
    gates = jax.nn.sigmoid(jnp.dot(xn, wg_ref[...], preferred_element_type=F32) + bg_ref[...])
    merged = (gates[:, :d] * jnp.dot(oa_ref[...], wa_ref[...], preferred_element_type=F32)
              + gates[:, d:] * jnp.dot(ob_ref[...], wb_ref[...], preferred_element_type=F32))
    h = x + jnp.dot(merged.astype(BF16), wo_ref[...], preferred_element_type=F32)
    hn = (h * lax.rsqrt(jnp.mean(h * h, axis=-1, keepdims=True) + RMS_EPS) * gffn_ref[...]).astype(BF16)
    hidden = wdn_ref.shape[0]
    out = h
    for c in range(hidden // ffn_chunk):
        lo = c * ffn_chunk
        gate = jnp.dot(hn, wup_ref[:, lo:lo + ffn_chunk], preferred_element_type=F32)
        up = jnp.dot(hn, wup_ref[:, hidden + lo:hidden + lo + ffn_chunk], preferred_element_type=F32)
        act = (gate * jax.nn.sigmoid(gate) * up).astype(BF16)
        out = out + jnp.dot(act, wdn_ref[lo:lo + ffn_chunk, :], preferred_element_type=F32)
    y = out * lax.rsqrt(jnp.mean(out * out, axis=-1, keepdims=True) + RMS_EPS) * gfin_ref[...]
    y_ref[...] = y


def _merge_ffn(x2d, oa, ob, weights, tm):
    n, d = x2d.shape
    row = lambda i: (i, 0)
    in_specs = [pl.BlockSpec((tm, d), row), pl.BlockSpec((tm, ATT_WIDTH), row),
                pl.BlockSpec((tm, ATT_WIDTH), row)] + [_const_spec(w.shape) for w in weights]
    hidden = weights[-2].shape[0]
    return pl.pallas_call(
        functools.partial(_merge_ffn_kernel, math.gcd(hidden, 2 * LANES)),
        grid=(n // tm,),
        in_specs=in_specs,
        out_specs=pl.BlockSpec((tm, d), row),
        out_shape=jax.ShapeDtypeStruct((n, d), F32),
        compiler_params=pltpu.CompilerParams(dimension_semantics=("arbitrary",),
                                             vmem_limit_bytes=VMEM_LIMIT),
    )(x2d, oa, ob, *weights)


def _top_lanes(g, count):
    lane = lax.broadcasted_iota(jnp.int32, g.shape, 1)
    out = jnp.zeros(g.shape, jnp.int32)
    for k in range(count):
        best = jnp.max(g, axis=-1, keepdims=True)
        idx = jnp.min(jnp.where(g == best, lane, LANES), axis=-1, keepdims=True)
        out = jnp.where(lane == k, idx, out)
        g = jnp.where(lane == idx, NEG_INF, g)
    return out


def _stream_kernel(ppb, pt_ref, qb_ref, qa_ref, knew_ref, vnew_ref, bnew_ref, bias_ref, lam_ref,
                   subln_ref, *refs):
    del pt_ref
    pps = PAGES_PER_STEP
    kb_refs, vb_refs, ka_refs = refs[:pps], refs[pps:2 * pps], refs[2 * pps:3 * pps]
    ob_ref, idx_ref, m_ref, l_ref, acc_ref, gate_ref = refs[3 * pps:]
    c = pl.program_id(1)
    nc = pl.num_programs(1)
    qb = qb_ref[0]
    half = qb.shape[0] // 2

    @pl.when(c == 0)
    def _():
        s = _nt(qb, knew_ref[0]) + bnew_ref[...]
        m = jnp.max(s, axis=-1, keepdims=True)
        p = jnp.exp(s - m)
        m_ref[...] = m
        l_ref[...] = jnp.sum(p, axis=-1, keepdims=True)
        acc_ref[...] = jnp.dot(p, vnew_ref[0], preferred_element_type=F32)
        gate_ref[...] = jnp.full(gate_ref.shape, NEG_INF, F32)

    s = jnp.concatenate([_nt(qb, kr[...]) for kr in kb_refs], axis=-1) + bias_ref[0]
    m_old = m_ref[...]
    m_new = jnp.maximum(m_old, jnp.max(s, axis=-1, keepdims=True))
    p = jnp.exp(s - m_new)
    alpha = jnp.exp(m_old - m_new)
    l_ref[...] = alpha * l_ref[...] + jnp.sum(p, axis=-1, keepdims=True)
    width = kb_refs[0].shape[0]
    pv = jnp.dot(p[:, :width], vb_refs[0][...], preferred_element_type=F32)
    for j in range(1, pps):
        pv = pv + jnp.dot(p[:, j * width:(j + 1) * width], vb_refs[j][...], preferred_element_type=F32)
    acc_ref[...] = alpha * acc_ref[...] + pv
    m_ref[...] = m_new

    qa = qa_ref[0]
    gate = gate_ref[...]
    lane = lax.broadcasted_iota(jnp.int32, gate.shape, 1)
    for j in range(pps // ppb):
        sa = jnp.dot(qa, ka_refs[j * ppb][...], preferred_element_type=F32)
        for jj in range(1, ppb):
            sa = sa + jnp.dot(qa, ka_refs[j * ppb + jj][...], preferred_element_type=F32)
        g = jnp.sum(sa, axis=-1, keepdims=True)
        gate = jnp.where(lane == c * (pps // ppb) + j, g, gate)
    gate_ref[...] = gate

    @pl.when(c == nc - 1)
    def _():
        lp = lam_ref[...]
        lam = (jnp.exp(jnp.sum(lp[0:1] * lp[1:2], axis=-1, keepdims=True))
               - jnp.exp(jnp.sum(lp[2:3] * lp[3:4], axis=-1, keepdims=True)) + LAMBDA_INIT)
        o = acc_ref[...] / l_ref[...]
        out = o[:half] - lam * o[half:]
        out = out * lax.rsqrt(jnp.mean(out * out, axis=-1, keepdims=True) + RMS_EPS)
        ob_ref[0] = out * subln_ref[...] * (1.0 - LAMBDA_INIT)
        idx_ref[0] = _top_lanes(gate, MOBA_TOPK)


def _sample_stream(page_table, qb_rows, qa_rows, knew, vnew, bnew, bias, lam_params, subln,
                   pool_kb, pool_vb, pool_ka_t, page):
    db, n_pages = page_table.shape
    pps = PAGES_PER_STEP
    nc = n_pages // pps
    rows_b, rows_a = qb_rows.shape[1], qa_rows.shape[1]
    blk_b, blk_a = pool_kb.shape[1:], pool_ka_t.shape[1:]

    def page_spec(j, blk):
        return pl.BlockSpec((None,) + blk, lambda b, c, pt: (pt[b * n_pages + c * pps + j], 0, 0))

    per_b = lambda b, c, pt: (b, 0, 0)
    fixed = lambda b, c, pt: (0, 0)
    in_specs = [
        pl.BlockSpec((1,) + qb_rows.shape[1:], per_b),
        pl.BlockSpec((1,) + qa_rows.shape[1:], per_b),
        pl.BlockSpec((1,) + knew.shape[1:], per_b),
        pl.BlockSpec((1,) + vnew.shape[1:], per_b),
        pl.BlockSpec(bnew.shape, fixed),
        pl.BlockSpec((1,) + bias.shape[1:], lambda b, c, pt: (jnp.where(c == nc - 1, 1, 0), 0, 0)),
        pl.BlockSpec(lam_params.shape, fixed),
        pl.BlockSpec(subln.shape, fixed),
    ] + [page_spec(j, blk_b) for j in range(pps)] * 2 + [page_spec(j, blk_a) for j in range(pps)]
    grid_spec = pltpu.PrefetchScalarGridSpec(
        num_scalar_prefetch=1,
        grid=(db, nc),
        in_specs=in_specs,
        out_specs=[pl.BlockSpec((1, rows_b // 2, LANES), per_b), pl.BlockSpec((1, rows_a, LANES), per_b)],
        scratch_shapes=[pltpu.VMEM((rows_b, 1), F32), pltpu.VMEM((rows_b, 1), F32),
                        pltpu.VMEM((rows_b, LANES), F32), pltpu.VMEM((rows_a, LANES), F32)],
    )
    return pl.pallas_call(
        functools.partial(_stream_kernel, MOBA_BLOCK // page),
        grid_spec=grid_spec,
        out_shape=[jax.ShapeDtypeStruct((db, rows_b // 2, LANES), F32),
                   jax.ShapeDtypeStruct((db, rows_a, LANES), jnp.int32)],
        compiler_params=pltpu.CompilerParams(dimension_semantics=("arbitrary", "arbitrary"),
                                             vmem_limit_bytes=VMEM_LIMIT),
    )(page_table.reshape(-1), qb_rows, qa_rows, knew, vnew, bnew, bias, lam_params, subln,
      *([pool_kb] * pps), *([pool_vb] * pps), *([pool_ka_t] * pps))


def _moba_sample_kernel(ppb, last_block, pt_ref, sel_ref, q_ref, knew_ref, vnew_ref, bown_ref,
                        bnear_ref, cfar_ref, *refs):
    del pt_ref
    n_sel = ppb * MOBA_TOPK
    k_refs, v_refs = refs[:n_sel], refs[n_sel:2 * n_sel]
    o_ref = refs[2 * n_sel]
    b, h, t = pl.program_id(0), pl.program_id(1), pl.program_id(2)
    n_heads, t_new = pl.num_programs(1), pl.num_programs(2)

    @pl.when(t == 0)
    def _():
        o_ref[...] = jnp.zeros_like(o_ref)

    q1 = q_ref[0, 0, pl.ds(t, 1), :]
    q8 = jnp.broadcast_to(q1, (8, HEAD_DIM))
    s_own = jnp.sum(q1 * knew_ref[0, 0], axis=-1, keepdims=True) + bown_ref[0, 0]
    scores = []
    for j in range(n_sel):
        bias = cfar_ref[0]
        if j % ppb == ppb - 1:
            blk = sel_ref[((b * n_heads + h) * t_new + t) * MOBA_TOPK + j // ppb]
            bias = jnp.where(blk == last_block, bnear_ref[0, pl.ds(t, 1), :], bias)
        scores.append(jnp.dot(q8, k_refs[j][...], preferred_element_type=F32) + bias)
    s = jnp.concatenate(scores, axis=-1)
    m = jnp.maximum(jnp.max(s, axis=-1, keepdims=True)[0:1], jnp.max(s_own, axis=0, keepdims=True))
    p = jnp.exp(s - m)
    p_own = jnp.exp(s_own - m)
    l = jnp.sum(p, axis=-1, keepdims=True)[0:1] + jnp.sum(p_own, axis=0, keepdims=True)
    acc = jnp.sum(p_own * vnew_ref[0, 0], axis=0, keepdims=True)
    page = k_refs[0].shape[1]
    for j in range(n_sel):
        acc = acc + _nt(p[:, j * page:(j + 1) * page], v_refs[j][...])[0:1]
    o_ref[0, 0, pl.ds(t, 1), :] = acc / l


def _moba_sample(page_table, sel, q8, knew8, vnew8, bown, bnear, cfar, pool_k_t, pool_v_t):
    db, n_pages = page_table.shape
    page = pool_k_t.shape[3]
    t_new = sel.shape[2]
    ppb = MOBA_BLOCK // page
    last_block = n_pages // ppb - 1

    def page_spec(j):
        k, p = j // ppb, j % ppb

        def index(b, h, t, pt, sl):
            blk = sl[((b * MOBA_HEADS + h) * t_new + t) * MOBA_TOPK + k]
            return (pt[b * n_pages + blk * ppb + p], h, 0, 0)
        return pl.BlockSpec((None, None, HEAD_DIM, page), index)

    n_sel = MOBA_TOPK * ppb
    per_bh = lambda b, h, t, pt, sl: (b, h, 0, 0)
    in_specs = [
        pl.BlockSpec((1, 1, 8, HEAD_DIM), per_bh),
        pl.BlockSpec((1, 1, 8, HEAD_DIM), per_bh),
        pl.BlockSpec((1, 1, 8, HEAD_DIM), per_bh),
        pl.BlockSpec((1, 1, 8, 1), lambda b, h, t, pt, sl: (h, t, 0, 0)),
        pl.BlockSpec((1, 8, page), lambda b, h, t, pt, sl: (h, 0, 0)),
        pl.BlockSpec((1, 1, page), lambda b, h, t, pt, sl: (h, 0, 0)),
    ] + [page_spec(j) for j in range(n_sel)] * 2
    grid_spec = pltpu.PrefetchScalarGridSpec(
        num_scalar_prefetch=2,
        grid=(db, MOBA_HEADS, t_new),
        in_specs=in_specs,
        out_specs=pl.BlockSpec((1, 1, 8, HEAD_DIM), per_bh),
    )
    return pl.pallas_call(
        functools.partial(_moba_sample_kernel, ppb, last_block),
        grid_spec=grid_spec,
        out_shape=jax.ShapeDtypeStruct((db, MOBA_HEADS, 8, HEAD_DIM), F32),
        compiler_params=pltpu.CompilerParams(
            dimension_semantics=("arbitrary", "arbitrary", "arbitrary")),
    )(page_table.reshape(-1), sel.reshape(-1), q8, knew8, vnew8, bown, bnear, cfar,
      *([pool_k_t] * n_sel), *([pool_v_t] * n_sel))


def kernel(x_prompt, x_sample, cache_k_a, cache_v_a, cache_k_b, cache_v_b, page_table, g_mix, w_in,
           w_merge_gate, b_merge_gate, w_branch_a, w_branch_b, w_out, lambda_q1, lambda_k1, lambda_q2,
           lambda_k2, subln_w, g_ffn, w_ffn_up, w_ffn_down, rel_bias, g_final):
    batch, seq, d = x_prompt.shape
    db, t_new, _ = x_sample.shape
    depth, n_pool, page = cache_k_a.shape[:3]
    n_pages = page_table.shape[1]
    past = n_pages * page
    nb_past = past // MOBA_BLOCK
    w = ATT_WIDTH
    assert depth == 1 and seq % TQ == 0 and past % MOBA_BLOCK == 0 and MOBA_BLOCK % page == 0
    assert n_pages % PAGES_PER_STEP == 0 and MOBA_TOPK <= nb_past <= LANES and t_new <= 8
    assert page == LANES and PAGES_PER_STEP % (MOBA_BLOCK // page) == 0

    tab = rel_bias.astype(F32).T
    tab_a, tab_b = tab[:MOBA_HEADS], tab[MOBA_HEADS:]
    far_a, far_b = tab_a[:, REL_BUCKETS - 1], tab_b[:, REL_BUCKETS - 1]
    lam_params = jnp.concatenate([lambda_q1, lambda_k1, lambda_q2, lambda_k2], axis=0).astype(F32)
    subln = subln_w.astype(F32).reshape(1, 2 * HEAD_DIM)
    w_in_b = w_in[0].astype(BF16)
    merge_weights = (g_mix.reshape(1, d), w_merge_gate[0].astype(BF16), b_merge_gate.reshape(1, 2 * d),
                     w_branch_a[0].astype(BF16), w_branch_b[0].astype(BF16), w_out[0].astype(BF16),
                     g_ffn.reshape(1, d), w_ffn_up[0].astype(BF16), w_ffn_down[0].astype(BF16),
                     g_final.reshape(1, d))

    xp = x_prompt.reshape(batch * seq, d)
    qk, ka, va, kb, vb, vat, vbt, kmean = _project(xp, g_mix, w_in_b, TQ)
    bias_a, cfar_a = _prompt_bias(tab_a, [(2 * u, 2 * u + 1) for u in range(UNITS)])
    bias_b, cfar_b = _prompt_bias(tab_b, [(u, u) for u in range(UNITS)])
    o_a = _prompt_attention(True, qk, vat, 0, batch, seq, bias_a, cfar_a,
                            (kmean.reshape(batch, seq // TQ, w),))
    o_b = _prompt_attention(False, qk, vbt, 2, batch, seq, bias_b, cfar_b,
                            (lam_params, subln.reshape(2 * HEAD_DIM, 1)))
    y_prompt = _merge_ffn(xp, o_a, o_b, merge_weights, TQ).reshape(batch, seq, d)

    n_s = db * t_new
    xs = x_sample.reshape(n_s, d)
    qk_s, ka_s, va_s, kb_s, vb_s, _, _, _ = _project(xs, g_mix, w_in_b, n_s)
    qk_s3 = qk_s.reshape(db, t_new, 4 * w).astype(F32)
    pad_to = lambda a, axis, n: jnp.pad(a, [(0, n - a.shape[i] if i == axis else 0) for i in range(a.ndim)])
    hb = DIFF_HEADS
    pool_kb = cache_k_b.reshape(n_pool, page * hb, 2 * HEAD_DIM)
    pool_vb = cache_v_b.reshape(n_pool, page * hb, 2 * HEAD_DIM)
    pool_ka_t = jnp.transpose(cache_k_a[0], (0, 2, 3, 1))
    pool_va_t = jnp.transpose(cache_v_a[0], (0, 2, 3, 1))

    lane = np.arange(LANES)
    qb16 = qk_s3[:, :, 2 * w:3 * w].reshape(db, t_new, hb, LANES).transpose(0, 2, 1, 3)
    qb16 = qb16.reshape(db, hb * t_new, LANES)
    qb_rows = jnp.concatenate([jnp.where(lane < HEAD_DIM, qb16, 0.0),
                               jnp.where(lane >= HEAD_DIM, qb16, 0.0)], axis=1)
    rows_b = 2 * hb * t_new
    head_of_row = (np.arange(rows_b) // t_new) % hb
    tok_of_row = np.arange(rows_b) % t_new
    head_mask = lambda cols: jnp.asarray(head_of_row[:, None] == (cols % hb)[None, :])
    tab_rows = tab_b[head_of_row]
    cols_new = np.arange(LANES)
    d_new = np.where(cols_new[None, :] < t_new * hb, tok_of_row[:, None] - (cols_new // hb)[None, :], -1)
    bnew = jnp.where(head_mask(cols_new), _bias_from_dist(tab_rows, d_new, per_row=True), NEG_INF)
    span = PAGES_PER_STEP * page
    cols = np.arange(span * hb)
    d_last = (past + tok_of_row)[:, None] - (past - span + cols // hb)[None, :]
    bias_s = jnp.stack([jnp.broadcast_to(far_b[head_of_row][:, None], (rows_b, span * hb)),
                        _bias_from_dist(tab_rows, d_last, per_row=True)])
    bias_s = jnp.where(head_mask(cols)[None], bias_s, NEG_INF)
    seg = np.arange(w) // HEAD_DIM
    qa_s = qk_s3[:, :, 0:w]
    qa_rows = jnp.where(jnp.asarray(seg[None, None, :] == np.arange(MOBA_HEADS)[:, None, None]),
                        qa_s[:, None], 0.0).reshape(db, MOBA_HEADS * t_new, w)
    knew_b = pad_to(kb_s.reshape(db, t_new * hb, 2 * HEAD_DIM), 1, LANES)
    vnew_b = pad_to(vb_s.reshape(db, t_new * hb, 2 * HEAD_DIM), 1, LANES)
    o_b16, sel_rows = _sample_stream(
        page_table, qb_rows, qa_rows, knew_b, vnew_b, bnew, bias_s, lam_params, subln,
        pool_kb, pool_vb, pool_ka_t.reshape(n_pool, w, page), page)
    o_b_s = o_b16.reshape(db, hb, t_new, LANES).transpose(0, 2, 1, 3).reshape(n_s, w).astype(BF16)
    sel = sel_rows[:, :, :MOBA_TOPK].reshape(db, MOBA_HEADS, t_new, MOBA_TOPK)

    heads_first = lambda a: pad_to(a.reshape(db, t_new, MOBA_HEADS, HEAD_DIM).transpose(0, 2, 1, 3), 2, 8)
    d_own = np.arange(t_new)[:, None] - np.arange(8)[None, :]
    d_own = np.where(np.arange(8)[None, :] < t_new, d_own, -1)
    bown = _bias_from_dist(tab_a, d_own)[..., None]
    d_near = (past + np.arange(8))[:, None] - (past - page + np.arange(page))[None, :]
    bnear = _bias_from_dist(tab_a, d_near)
    cfar_s = jnp.broadcast_to(far_a[:, None, None], (MOBA_HEADS, 1, page))
    o_heads = _moba_sample(page_table, sel, heads_first(qa_s), heads_first(ka_s), heads_first(va_s),
                           bown, bnear, cfar_s, pool_ka_t, pool_va_t)
    o_a_s = o_heads[:, :, :t_new].transpose(0, 2, 1, 3).reshape(n_s, w).astype(BF16)
    y_sample = _merge_ffn(xs, o_a_s, o_b_s, merge_weights, n_s).reshape(db, t_new, d)

    a_shape = (MOBA_HEADS, HEAD_DIM)
    b_shape = (DIFF_HEADS, 2 * HEAD_DIM)
    return (y_prompt, y_sample,
            ka.reshape((1, batch, seq) + a_shape), va.reshape((1, batch, seq) + a_shape),
            kb.reshape((1, batch, seq) + b_shape), vb.reshape((1, batch, seq) + b_shape),
            ka_s.reshape((1, db, t_new) + a_shape), va_s.reshape((1, db, t_new) + a_shape),
            kb_s.reshape((1, db, t_new) + b_shape), vb_s.reshape((1, db, t_new) + b_shape))
```

```python
import functools
import math

import numpy as np
import jax
import jax.numpy as jnp
from jax import lax
from jax.experimental import pallas as pl
from jax.experimental.pallas import tpu as pltpu

HEAD_DIM = 64
MOBA_HEADS = 8
DIFF_HEADS = 4
MOBA_BLOCK = 256
MOBA_TOPK = 3
REL_BUCKETS = 32
REL_MAX_DIST = 128
RMS_EPS = 1e-6
LAMBDA_INIT = 0.8 - 0.6 * math.exp(-0.3 * 0)

LANES = 128
ATT_WIDTH = 512
UNITS = ATT_WIDTH // LANES
TQ = MOBA_BLOCK
PAGES_PER_STEP = 8
VMEM_LIMIT = 56 * 1024 * 1024

F32 = jnp.float32
BF16 = jnp.bfloat16
NEG_INF = float("-inf")


def _rel_bucket_np(dist):
    n = np.maximum(dist, 0)
    exact = REL_BUCKETS // 2
    nf = np.maximum(n, 1).astype(np.float64)
    large = exact + (np.log(nf / exact) / math.log(REL_MAX_DIST / exact)
                     * (REL_BUCKETS - exact)).astype(np.int64)
    return np.where(n < exact, n, np.minimum(large, REL_BUCKETS - 1)).astype(np.int32)


def _bias_from_dist(tab, dist, per_row=False):
    idx = jnp.asarray(_rel_bucket_np(dist))
    col = (lambda k: tab[:, k:k + 1]) if per_row else (
        lambda k: tab[:, k].reshape((-1,) + (1,) * dist.ndim))
    b = jnp.broadcast_to(col(0), dist.shape if per_row else (tab.shape[0],) + dist.shape)
    for k in range(1, REL_BUCKETS):
        b = jnp.where(idx == k, col(k), b)
    return jnp.where(jnp.asarray(dist >= 0), b, NEG_INF)


def _nt(a, b, **kw):
    return lax.dot_general(a, b, (((1,), (1,)), ((), ())), preferred_element_type=F32, **kw)


def _const_spec(shape):
    n = len(shape)
    return pl.BlockSpec(shape, lambda *_: (0,) * n, pipeline_mode=pl.Buffered(1))


def _proj_kernel(x_ref, g_ref, w_ref, qk_ref, kat_ref, vat_ref, kb_ref, vb_ref, vat16_ref, vbt16_ref,
                 kmean_ref):
    x = x_ref[...]
    xn = x * lax.rsqrt(jnp.mean(x * x, axis=-1, keepdims=True) + RMS_EPS) * g_ref[...]
    p = jnp.dot(xn.astype(BF16), w_ref[...], preferred_element_type=F32)
    w = ATT_WIDTH
    scale = HEAD_DIM ** -0.5
    ka, va = p[:, w:2 * w], p[:, 2 * w:3 * w]
    kb, vb = p[:, 4 * w:5 * w], p[:, 5 * w:6 * w]
    vat = va.T
    kat_ref[0] = ka.T
    vat_ref[0] = vat
    kb_ref[...] = kb
    vb_ref[...] = vb
    qk_ref[:, 0:w] = (p[:, 0:w] * scale).astype(BF16)
    qk_ref[:, w:2 * w] = ka.astype(BF16)
    qk_ref[:, 2 * w:3 * w] = (p[:, 3 * w:4 * w] * scale).astype(BF16)
    qk_ref[:, 3 * w:4 * w] = kb.astype(BF16)
    vat16_ref[0] = vat.astype(BF16)
    vbt16_ref[0] = vb.T.astype(BF16)
    kmean_ref[0] = jnp.mean(ka, axis=0, keepdims=True)


def _project(x2d, g, w_bf16, tm, tiles_per_seq):
    n, d = x2d.shape
    w = ATT_WIDTH
    steps = n // tm
    seq = tm * tiles_per_seq
    row = lambda i: (i, 0)
    f32_rows = jax.ShapeDtypeStruct((n, w), F32)
    f32_t = jax.ShapeDtypeStruct((n // seq, w, seq), F32)
    t_spec = pl.BlockSpec((1, w, tm), lambda i: (i // tiles_per_seq, 0, i % tiles_per_seq))
    return pl.pallas_call(
        _proj_kernel,
        grid=(steps,),
        in_specs=[pl.BlockSpec((tm, d), row), _const_spec((1, d)), _const_spec((d, 6 * w))],
        out_specs=[pl.BlockSpec((tm, 4 * w), row), t_spec, t_spec,
                   pl.BlockSpec((tm, w), row), pl.BlockSpec((tm, w), row)]
                  + [pl.BlockSpec((1, w, tm), lambda i: (i, 0, 0))] * 2
                  + [pl.BlockSpec((1, 1, w), lambda i: (i, 0, 0))],
        out_shape=[jax.ShapeDtypeStruct((n, 4 * w), BF16), f32_t, f32_t, f32_rows, f32_rows,
                   jax.ShapeDtypeStruct((steps, w, tm), BF16), jax.ShapeDtypeStruct((steps, w, tm), BF16),
                   jax.ShapeDtypeStruct((steps, 1, w), F32)],
        compiler_params=pltpu.CompilerParams(dimension_semantics=("arbitrary",),
                                             vmem_limit_bytes=VMEM_LIMIT),
    )(x2d, g.reshape(1, d), w_bf16)


def _split_halves(q2):
    lane = lax.broadcasted_iota(jnp.int32, q2.shape, 1)
    zero = jnp.zeros_like(q2)
    return jnp.concatenate([jnp.where(lane < HEAD_DIM, q2, zero),
                            jnp.where(lane >= HEAD_DIM, q2, zero)], axis=0)


def _online_update(u, s, shift, vt, m_ref, l_ref, acc_ref):
    m_old = m_ref[u]
    m_new = jnp.maximum(m_old, jnp.max(s, axis=0, keepdims=True) + shift)
    p = jnp.exp(s - (m_new - shift))
    alpha = jnp.exp(m_old - m_new)
    l_ref[u] = alpha * l_ref[u] + jnp.sum(p, axis=0, keepdims=True)
    acc_ref[u] = alpha * acc_ref[u] + jnp.dot(vt, p.astype(BF16), preferred_element_type=F32)
    m_ref[u] = m_new


def _units_ahead(score_fn, fold_fn, first=None, after=None):
    s_next = score_fn(0) if first is None else first
    for u in range(UNITS):
        s = s_next
        if u + 1 < UNITS:
            s_next = score_fn(u + 1)
        elif after is not None:
            s_next = after()
        fold_fn(u, s)
    return s_next


def _attn_kernel(moba, *refs):
    if moba:
        (q_ref, k_ref, vt_ref, bias_ref, cfar_ref, kmean_ref, o_ref,
         qs_ref, m_ref, l_ref, acc_ref, sh_ref) = refs
    else:
        (q_ref, k_ref, vt_ref, bias_ref, cfar_ref, lam_ref, subln_ref, o_ref,
         qs_ref, m_ref, l_ref, acc_ref) = refs
    i = pl.program_id(1)
    tq = TQ

    def k_tile(n, u):
        return k_ref[pl.ds(pl.multiple_of(n * tq, tq), tq), u * LANES:(u + 1) * LANES]

    def vt_tile(n, u):
        return vt_ref[n, u * LANES:(u + 1) * LANES, :]

    def scores(n):
        return lambda u: _nt(k_tile(n, u), qs_ref[u])

    for u in range(UNITS):
        qs = _split_halves(q_ref[:, u * LANES:(u + 1) * LANES])
        qs_ref[u] = qs
        if moba:
            km = kmean_ref[0, :, u * LANES:(u + 1) * LANES]
            hi = km.astype(BF16)
            r1 = km - hi.astype(F32)
            mid = r1.astype(BF16)
            lo = (r1 - mid.astype(F32)).astype(BF16)
            g = _nt(hi, qs) + _nt(mid, qs) + _nt(lo, qs)
            nb = g.shape[0]
            blk = lax.broadcasted_iota(jnp.int32, g.shape, 0)
            valid = blk < i
            g = jnp.where(valid, g, NEG_INF)
            rank = jnp.zeros(g.shape, jnp.int32)
            for mm in range(nb):
                gm = g[mm:mm + 1, :]
                ahead = (gm > g) | ((gm == g) & (mm < blk))
                rank = rank + ahead.astype(jnp.int32)
            sel = valid & (rank < MOBA_TOPK)
            sh_ref[u] = jnp.where(sel, cfar_ref[u], NEG_INF)

    def fold_own(u, s):
        s = s + bias_ref[u, 0]
        m = jnp.max(s, axis=0, keepdims=True)
        p = jnp.exp(s - m)
        m_ref[u] = m
        l_ref[u] = jnp.sum(p, axis=0, keepdims=True)
        acc_ref[u] = jnp.dot(vt_tile(i, u), p.astype(BF16), preferred_element_type=F32)

    _units_ahead(scores(i), fold_own)

    @pl.when(i >= 1)
    def _():
        def fold_prev(u, s):
            if moba:
                shift = sh_ref[u, pl.ds(i - 1, 1), :] - cfar_ref[u]
            else:
                shift = jnp.zeros((1, 2 * tq), F32)
            _online_update(u, s + bias_ref[u, 1], shift, vt_tile(i - 1, u), m_ref, l_ref, acc_ref)

        _units_ahead(scores(i - 1), fold_prev)

    def far(n, s_first):
        def fold_far(u, s):
            shift = sh_ref[u, pl.ds(n, 1), :] if moba else cfar_ref[u]
            _online_update(u, s, shift, vt_tile(n, u), m_ref, l_ref, acc_ref)

        nxt = jnp.minimum(n + 1, i - 2)
        return _units_ahead(scores(n), fold_far, first=s_first, after=lambda: scores(nxt)(0))

    lax.fori_loop(0, i - 1, far, scores(0)(0))

    chan = lax.broadcasted_iota(jnp.int32, (LANES, tq), 0)
    if not moba:
        lp = lam_ref[...]
        lam = (jnp.exp(jnp.sum(lp[0:1] * lp[1:2], axis=-1, keepdims=True))
               - jnp.exp(jnp.sum(lp[2:3] * lp[3:4], axis=-1, keepdims=True)) + LAMBDA_INIT)
    for u in range(UNITS):
        o = acc_ref[u] / l_ref[u]
        if moba:
            out = jnp.where(chan < HEAD_DIM, o[:, :tq], o[:, tq:])
        else:
            out = o[:, :tq] - lam * o[:, tq:]
            out = out * lax.rsqrt(jnp.mean(out * out, axis=0, keepdims=True) + RMS_EPS)
            out = out * subln_ref[...] * (1.0 - LAMBDA_INIT)
        o_ref[:, u * LANES:(u + 1) * LANES] = out.T.astype(o_ref.dtype)


def _prompt_attention(moba, qk, vt, col0, batch, seq, bias, cfar, extra):
    nt = seq // TQ
    w = ATT_WIDTH
    in_specs = [
        pl.BlockSpec((TQ, w), lambda b, i: (b * nt + i, col0)),
        pl.BlockSpec((seq, w), lambda b, i: (b, col0 + 1)),
        pl.BlockSpec((nt, w, TQ), lambda b, i: (b, 0, 0)),
        _const_spec(bias.shape),
        _const_spec(cfar.shape),
    ]
    scratch = [pltpu.VMEM((UNITS, 2 * TQ, LANES), BF16),
               pltpu.VMEM((UNITS, 1, 2 * TQ), F32),
               pltpu.VMEM((UNITS, 1, 2 * TQ), F32),
               pltpu.VMEM((UNITS, LANES, 2 * TQ), F32)]
    if moba:
        (kmean,) = extra
        in_specs.append(pl.BlockSpec((1, nt, w), lambda b, i: (b, 0, 0)))
        scratch.append(pltpu.VMEM((UNITS, nt, 2 * TQ), F32))
    else:
        lam_params, subln_col = extra
        in_specs += [_const_spec(lam_params.shape), _const_spec(subln_col.shape)]
    return pl.pallas_call(
        functools.partial(_attn_kernel, moba),
        grid=(batch, nt),
        in_specs=in_specs,
        out_specs=pl.BlockSpec((TQ, w), lambda b, i: (b * nt + i, 0)),
        out_shape=jax.ShapeDtypeStruct((batch * seq, w), BF16),
        scratch_shapes=scratch,
        compiler_params=pltpu.CompilerParams(dimension_semantics=("arbitrary", "arbitrary"),
                                             vmem_limit_bytes=VMEM_LIMIT),
    )(qk, qk, vt, bias, cfar, *extra)


def _prompt_bias(tab, heads_of_unit):
    r = np.arange(TQ)[None, :] - np.arange(TQ)[:, None]
    own = _bias_from_dist(tab, r)
    prev = _bias_from_dist(tab, r + TQ)
    far = tab[:, REL_BUCKETS - 1]
    bias, cfar = [], []
    for h0, h1 in heads_of_unit:
        bias.append(jnp.stack([jnp.concatenate([own[h0], own[h1]], axis=1),
                               jnp.concatenate([prev[h0], prev[h1]], axis=1)]))
        cfar.append(jnp.concatenate([jnp.full((1, TQ), far[h0]), jnp.full((1, TQ), far[h1])], axis=1))
    return jnp.stack(bias), jnp.stack(cfar)


def _merge_ffn_kernel(ffn_chunk, x_ref, oa_ref, ob_ref, gmix_ref, wg_ref, bg_ref, wa_ref, wb_ref,
                      wo_ref, gffn_ref, wup_ref, wdn_ref, gfin_ref, y_ref):
    x = x_ref[...]
    d = x.shape[1]
    xn = (x * lax.rsqrt(jnp.mean(x * x, axis=-1, keepdims=True) + RMS_EPS) * gmix_ref[...]).astype(BF16)
    gates = jax.nn.sigmoid(jnp.dot(xn, wg_ref[...], preferred_element_type=F32) + bg_ref[...])
    merged = (gates[:, :d] * jnp.dot(oa_ref[...], wa_ref[...], preferred_element_type=F32)
              + gates[:, d:] * jnp.dot(ob_ref[...], wb_ref[...], preferred_element_type=F32))
    h = x + jnp.dot(merged.astype(BF16), wo_ref[...], preferred_element_type=F32)
    hn = (h * lax.rsqrt(jnp.mean(h * h, axis=-1, keepdims=True) + RMS_EPS) * gffn_ref[...]).astype(BF16)
    hidden = wdn_ref.shape[0]
    out = h
    for c in range(hidden // ffn_chunk):
        lo = c * ffn_chunk
        gate = jnp.dot(hn, wup_ref[:, lo:lo + ffn_chunk], preferred_element_type=F32)
        up = jnp.dot(hn, wup_ref[:, hidden + lo:hidden + lo + ffn_chunk], preferred_element_type=F32)
        act = (gate * jax.nn.sigmoid(gate) * up).astype(BF16)
        out = out + jnp.dot(act, wdn_ref[lo:lo + ffn_chunk, :], preferred_element_type=F32)
    y = out * lax.rsqrt(jnp.mean(out * out, axis=-1, keepdims=True) + RMS_EPS) * gfin_ref[...]
    y_ref[...] = y


def _merge_ffn(x2d, oa, ob, weights, tm):
    n, d = x2d.shape
    row = lambda i: (i, 0)
    in_specs = [pl.BlockSpec((tm, d), row), pl.BlockSpec((tm, ATT_WIDTH), row),
                pl.BlockSpec((tm, ATT_WIDTH), row)] + [_const_spec(w.shape) for w in weights]
    hidden = weights[-2].shape[0]
    return pl.pallas_call(
        functools.partial(_merge_ffn_kernel, math.gcd(hidden, 2 * LANES)),
        grid=(n // tm,),
        in_specs=in_specs,
        out_specs=pl.BlockSpec((tm, d), row),
        out_shape=jax.ShapeDtypeStruct((n, d), F32),
        compiler_params=pltpu.CompilerParams(dimension_semantics=("arbitrary",),
                                             vmem_limit_bytes=VMEM_LIMIT),
    )(x2d, oa, ob, *weights)


def _top_lanes(g, count):
    lane = lax.broadcasted_iota(jnp.int32, g.shape, 1)
    out = jnp.zeros(g.shape, jnp.int32)
    for k in range(count):
        best = jnp.max(g, axis=-1, keepdims=True)
        idx = jnp.min(jnp.where(g == best, lane, LANES), axis=-1, keepdims=True)
        out = jnp.where(lane == k, idx, out)
        g = jnp.where(lane == idx, NEG_INF, g)
    return out


def _stream_kernel(ppb, pt_ref, qb_ref, qa_ref, knew_ref, vnew_ref, bnew_ref, bias_ref, lam_ref,
                   subln_ref, *refs):
    del pt_ref
    pps = PAGES_PER_STEP
    kb_refs, vb_refs, ka_refs = refs[:pps], refs[pps:2 * pps], refs[2 * pps:3 * pps]
    ob_ref, idx_ref, m_ref, l_ref, acc_ref, gate_ref = refs[3 * pps:]
    c = pl.program_id(1)
    nc = pl.num_programs(1)
    qb = qb_ref[0]
    half = qb.shape[0] // 2

    @pl.when(c == 0)
    def _():
        s = _nt(qb, knew_ref[0]) + bnew_ref[...]
        m = jnp.max(s, axis=-1, keepdims=True)
        p = jnp.exp(s - m)
        m_ref[...] = m
        l_ref[...] = jnp.sum(p, axis=-1, keepdims=True)
        acc_ref[...] = jnp.dot(p, vnew_ref[0], preferred_element_type=F32)
        gate_ref[...] = jnp.full(gate_ref.shape, NEG_INF, F32)

    s = jnp.concatenate([_nt(qb, kr[...]) for kr in kb_refs], axis=-1) + bias_ref[0]
    m_old = m_ref[...]
    m_new = jnp.maximum(m_old, jnp.max(s, axis=-1, keepdims=True))
    p = jnp.exp(s - m_new)
    alpha = jnp.exp(m_old - m_new)
    l_ref[...] = alpha * l_ref[...] + jnp.sum(p, axis=-1, keepdims=True)
    width = kb_refs[0].shape[0]
    pv = jnp.dot(p[:, :width], vb_refs[0][...], preferred_element_type=F32)
    for j in range(1, pps):
        pv = pv + jnp.dot(p[:, j * width:(j + 1) * width], vb_refs[j][...], preferred_element_type=F32)
    acc_ref[...] = alpha * acc_ref[...] + pv
    m_ref[...] = m_new

    qa = qa_ref[0]
    gate = gate_ref[...]
    lane = lax.broadcasted_iota(jnp.int32, gate.shape, 1)
    for j in range(pps // ppb):
        sa = jnp.dot(qa, ka_refs[j * ppb][...], preferred_element_type=F32)
        for jj in range(1, ppb):
            sa = sa + jnp.dot(qa, ka_refs[j * ppb + jj][...], preferred_element_type=F32)
        g = jnp.sum(sa, axis=-1, keepdims=True)
        gate = jnp.where(lane == c * (pps // ppb) + j, g, gate)
    gate_ref[...] = gate

    @pl.when(c == nc - 1)
    def _():
        lp = lam_ref[...]
        lam = (jnp.exp(jnp.sum(lp[0:1] * lp[1:2], axis=-1, keepdims=True))
               - jnp.exp(jnp.sum(lp[2:3] * lp[3:4], axis=-1, keepdims=True)) + LAMBDA_INIT)
        o = acc_ref[...] / l_ref[...]
        out = o[:half] - lam * o[half:]
        out = out * lax.rsqrt(jnp.mean(out * out, axis=-1, keepdims=True) + RMS_EPS)
        ob_ref[0] = out * subln_ref[...] * (1.0 - LAMBDA_INIT)
        idx_ref[0] = _top_lanes(gate, MOBA_TOPK)


def _sample_stream(page_table, qb_rows, qa_rows, knew, vnew, bnew, bias, lam_params, subln,
                   pool_kb, pool_vb, pool_ka_t, page):
    db, n_pages = page_table.shape
    pps = PAGES_PER_STEP
    nc = n_pages // pps
    rows_b, rows_a = qb_rows.shape[1], qa_rows.shape[1]
    blk_b, blk_a = pool_kb.shape[1:], pool_ka_t.shape[1:]

    def page_spec(j, blk):
        return pl.BlockSpec((None,) + blk, lambda b, c, pt: (pt[b * n_pages + c * pps + j], 0, 0))

    per_b = lambda b, c, pt: (b, 0, 0)
    fixed = lambda b, c, pt: (0, 0)
    in_specs = [
        pl.BlockSpec((1,) + qb_rows.shape[1:], per_b),
        pl.BlockSpec((1,) + qa_rows.shape[1:], per_b),
        pl.BlockSpec((1,) + knew.shape[1:], per_b),
        pl.BlockSpec((1,) + vnew.shape[1:], per_b),
        pl.BlockSpec(bnew.shape, fixed),
        pl.BlockSpec((1,) + bias.shape[1:], lambda b, c, pt: (jnp.where(c == nc - 1, 1, 0), 0, 0)),
        pl.BlockSpec(lam_params.shape, fixed),
        pl.BlockSpec(subln.shape, fixed),
    ] + [page_spec(j, blk_b) for j in range(pps)] * 2 + [page_spec(j, blk_a) for j in range(pps)]
    grid_spec = pltpu.PrefetchScalarGridSpec(
        num_scalar_prefetch=1,
        grid=(db, nc),
        in_specs=in_specs,
        out_specs=[pl.BlockSpec((1, rows_b // 2, LANES), per_b), pl.BlockSpec((1, rows_a, LANES), per_b)],
        scratch_shapes=[pltpu.VMEM((rows_b, 1), F32), pltpu.VMEM((rows_b, 1), F32),
                        pltpu.VMEM((rows_b, LANES), F32), pltpu.VMEM((rows_a, LANES), F32)],
    )
    return pl.pallas_call(
        functools.partial(_stream_kernel, MOBA_BLOCK // page),
        grid_spec=grid_spec,
        out_shape=[jax.ShapeDtypeStruct((db, rows_b // 2, LANES), F32),
                   jax.ShapeDtypeStruct((db, rows_a, LANES), jnp.int32)],
        compiler_params=pltpu.CompilerParams(dimension_semantics=("arbitrary", "arbitrary"),
                                             vmem_limit_bytes=VMEM_LIMIT),
    )(page_table.reshape(-1), qb_rows, qa_rows, knew, vnew, bnew, bias, lam_params, subln,
      *([pool_kb] * pps), *([pool_vb] * pps), *([pool_ka_t] * pps))


def _moba_sample_kernel(ppb, last_block, t_new, pt_ref, sel_ref, q_ref, knew_ref, vnew_ref, bown_ref,
                        bnear_ref, cfar_ref, *refs):
    del pt_ref
    n_sel = ppb * MOBA_TOPK
    n_pages = t_new * n_sel
    k_refs, v_refs = refs[:n_pages], refs[n_pages:2 * n_pages]
    o_ref = refs[2 * n_pages]
    b, h = pl.program_id(0), pl.program_id(1)
    n_heads = pl.num_programs(1)
    page = k_refs[0].shape[1]

    o_ref[...] = jnp.zeros_like(o_ref)
    for t in range(t_new):
        q1 = q_ref[0, 0, t:t + 1, :]
        q8 = jnp.broadcast_to(q1, (8, HEAD_DIM))
        s_own = jnp.sum(q1 * knew_ref[0, 0], axis=-1, keepdims=True) + bown_ref[0, t]
        scores = []
        for j in range(n_sel):
            bias = cfar_ref[0]
            if j % ppb == ppb - 1:
                blk = sel_ref[((b * n_heads + h) * t_new + t) * MOBA_TOPK + j // ppb]
                bias = jnp.where(blk == last_block, bnear_ref[0, t:t + 1, :], bias)
            scores.append(jnp.dot(q8, k_refs[t * n_sel + j][...], preferred_element_type=F32) + bias)
        s = jnp.concatenate(scores, axis=-1)
        m = jnp.maximum(jnp.max(s, axis=-1, keepdims=True)[0:1], jnp.max(s_own, axis=0, keepdims=True))
        p = jnp.exp(s - m)
        p_own = jnp.exp(s_own - m)
        l = jnp.sum(p, axis=-1, keepdims=True)[0:1] + jnp.sum(p_own, axis=0, keepdims=True)
        acc = jnp.sum(p_own * vnew_ref[0, 0], axis=0, keepdims=True)
        for j in range(n_sel):
            acc = acc + _nt(p[:, j * page:(j + 1) * page], v_refs[t * n_sel + j][...])[0:1]
        o_ref[0, 0, t:t + 1, :] = acc / l


def _moba_sample(page_table, sel, q8, knew8, vnew8, bown, bnear, cfar, pool_k_t, pool_v_t):
    db, n_pages = page_table.shape
    page = pool_k_t.shape[3]
    t_new = sel.shape[2]
    ppb = MOBA_BLOCK // page
    last_block = n_pages // ppb - 1
    n_sel = MOBA_TOPK * ppb

    def page_spec(jt):
        t, k, p = jt // n_sel, (jt % n_sel) // ppb, jt % ppb

        def index(b, h, pt, sl):
            blk = sl[((b * MOBA_HEADS + h) * t_new + t) * MOBA_TOPK + k]
            return (pt[b * n_pages + blk * ppb + p], h, 0, 0)
        return pl.BlockSpec((None, None, HEAD_DIM, page), index)

    per_bh = lambda b, h, pt, sl: (b, h, 0, 0)
    in_specs = [
        pl.BlockSpec((1, 1, 8, HEAD_DIM), per_bh),
        pl.BlockSpec((1, 1, 8, HEAD_DIM), per_bh),
        pl.BlockSpec((1, 1, 8, HEAD_DIM), per_bh),
        pl.BlockSpec((1, t_new, 8, 1), lambda b, h, pt, sl: (h, 0, 0, 0)),
        pl.BlockSpec((1, 8, page), lambda b, h, pt, sl: (h, 0, 0)),
        pl.BlockSpec((1, 1, page), lambda b, h, pt, sl: (h, 0, 0)),
    ] + [page_spec(jt) for jt in range(t_new * n_sel)] * 2
    grid_spec = pltpu.PrefetchScalarGridSpec(
        num_scalar_prefetch=2,
        grid=(db, MOBA_HEADS),
        in_specs=in_specs,
        out_specs=pl.BlockSpec((1, 1, 8, HEAD_DIM), per_bh),
    )
    return pl.pallas_call(
        functools.partial(_moba_sample_kernel, ppb, last_block, t_new),
        grid_spec=grid_spec,
        out_shape=jax.ShapeDtypeStruct((db, MOBA_HEADS, 8, HEAD_DIM), F32),
        compiler_params=pltpu.CompilerParams(dimension_semantics=("arbitrary", "arbitrary")),
    )(page_table.reshape(-1), sel.reshape(-1), q8, knew8, vnew8, bown, bnear, cfar,
      *([pool_k_t] * (t_new * n_sel)), *([pool_v_t] * (t_new * n_sel)))


def kernel(x_prompt, x_sample, cache_k_a, cache_v_a, cache_k_b, cache_v_b, page_table, g_mix, w_in,
           w_merge_gate, b_merge_gate, w_branch_a, w_branch_b, w_out, lambda_q1, lambda_k1, lambda_q2,
           lambda_k2, subln_w, g_ffn, w_ffn_up, w_ffn_down, rel_bias, g_final):
    batch, seq, d = x_prompt.shape
    db, t_new, _ = x_sample.shape
    depth, n_pool, page = cache_k_a.shape[:3]
    n_pages = page_table.shape[1]
    past = n_pages * page
    nb_past = past // MOBA_BLOCK
    w = ATT_WIDTH
    assert depth == 1 and seq % TQ == 0 and past % MOBA_BLOCK == 0 and MOBA_BLOCK % page == 0
    assert n_pages % PAGES_PER_STEP == 0 and MOBA_TOPK <= nb_past <= LANES and t_new <= 8
    assert page == LANES and PAGES_PER_STEP % (MOBA_BLOCK // page) == 0

    tab = rel_bias.astype(F32).T
    tab_a, tab_b = tab[:MOBA_HEADS], tab[MOBA_HEADS:]
    far_a, far_b = tab_a[:, REL_BUCKETS - 1], tab_b[:, REL_BUCKETS - 1]
    lam_params = jnp.concatenate([lambda_q1, lambda_k1, lambda_q2, lambda_k2], axis=0).astype(F32)
    subln = subln_w.astype(F32).reshape(1, 2 * HEAD_DIM)
    w_in_b = w_in[0].astype(BF16)
    merge_weights = (g_mix.reshape(1, d), w_merge_gate[0].astype(BF16), b_merge_gate.reshape(1, 2 * d),
                     w_branch_a[0].astype(BF16), w_branch_b[0].astype(BF16), w_out[0].astype(BF16),
                     g_ffn.reshape(1, d), w_ffn_up[0].astype(BF16), w_ffn_down[0].astype(BF16),
                     g_final.reshape(1, d))

    xp = x_prompt.reshape(batch * seq, d)
    qk, ka_t, va_t, kb, vb, vat, vbt, kmean = _project(xp, g_mix, w_in_b, TQ, seq // TQ)
    bias_a, cfar_a = _prompt_bias(tab_a, [(2 * u, 2 * u + 1) for u in range(UNITS)])
    bias_b, cfar_b = _prompt_bias(tab_b, [(u, u) for u in range(UNITS)])
    o_a = _prompt_attention(True, qk, vat, 0, batch, seq, bias_a, cfar_a,
                            (kmean.reshape(batch, seq // TQ, w),))
    o_b = _prompt_attention(False, qk, vbt, 2, batch, seq, bias_b, cfar_b,
                            (lam_params, subln.reshape(2 * HEAD_DIM, 1)))
    y_prompt = _merge_ffn(xp, o_a, o_b, merge_weights, TQ).reshape(batch, seq, d)

    n_s = db * t_new
    xs = x_sample.reshape(n_s, d)
    qk_s, ka_st, va_st, kb_s, vb_s, _, _, _ = _project(xs, g_mix, w_in_b, n_s, 1)
    heads_first = lambda a: a.reshape(MOBA_HEADS, HEAD_DIM, db, t_new).transpose(2, 0, 3, 1)
    ka_s, va_s = heads_first(ka_st), heads_first(va_st)
    qk_s3 = qk_s.reshape(db, t_new, 4 * w).astype(F32)
    pad_to = lambda a, axis, n: jnp.pad(a, [(0, n - a.shape[i] if i == axis else 0) for i in range(a.ndim)])
    hb = DIFF_HEADS
    pool_kb = cache_k_b.reshape(n_pool, page * hb, 2 * HEAD_DIM)
    pool_vb = cache_v_b.reshape(n_pool, page * hb, 2 * HEAD_DIM)
    pool_ka_t = jnp.transpose(cache_k_a[0], (0, 2, 3, 1))
    pool_va_t = jnp.transpose(cache_v_a[0], (0, 2, 3, 1))

    lane = np.arange(LANES)
    qb16 = qk_s3[:, :, 2 * w:3 * w].reshape(db, t_new, hb, LANES).transpose(0, 2, 1, 3)
    qb16 = qb16.reshape(db, hb * t_new, LANES)
    qb_rows = jnp.concatenate([jnp.where(lane < HEAD_DIM, qb16, 0.0),
                               jnp.where(lane >= HEAD_DIM, qb16, 0.0)], axis=1)
    rows_b = 2 * hb * t_new
    head_of_row = (np.arange(rows_b) // t_new) % hb
    tok_of_row = np.arange(rows_b) % t_new
    head_mask = lambda cols: jnp.asarray(head_of_row[:, None] == (cols % hb)[None, :])
    tab_rows = tab_b[head_of_row]
    cols_new = np.arange(LANES)
    d_new = np.where(cols_new[None, :] < t_new * hb, tok_of_row[:, None] - (cols_new // hb)[None, :], -1)
    bnew = jnp.where(head_mask(cols_new), _bias_from_dist(tab_rows, d_new, per_row=True), NEG_INF)
    span = PAGES_PER_STEP * page
    cols = np.arange(span * hb)
    d_last = (past + tok_of_row)[:, None] - (past - span + cols // hb)[None, :]
    bias_s = jnp.stack([jnp.broadcast_to(far_b[head_of_row][:, None], (rows_b, span * hb)),
                        _bias_from_dist(tab_rows, d_last, per_row=True)])
    bias_s = jnp.where(head_mask(cols)[None], bias_s, NEG_INF)
    seg = np.arange(w) // HEAD_DIM
    qa_s = qk_s3[:, :, 0:w]
    qa_rows = jnp.where(jnp.asarray(seg[None, None, :] == np.arange(MOBA_HEADS)[:, None, None]),
                        qa_s[:, None], 0.0).reshape(db, MOBA_HEADS * t_new, w)
    knew_b = pad_to(kb_s.reshape(db, t_new * hb, 2 * HEAD_DIM), 1, LANES)
    vnew_b = pad_to(vb_s.reshape(db, t_new * hb, 2 * HEAD_DIM), 1, LANES)
    o_b16, sel_rows = _sample_stream(
        page_table, qb_rows, qa_rows, knew_b, vnew_b, bnew, bias_s, lam_params, subln,
        pool_kb, pool_vb, pool_ka_t.reshape(n_pool, w, page), page)
    o_b_s = o_b16.reshape(db, hb, t_new, LANES).transpose(0, 2, 1, 3).reshape(n_s, w).astype(BF16)
    sel = sel_rows[:, :, :MOBA_TOPK].reshape(db, MOBA_HEADS, t_new, MOBA_TOPK)

    q_heads = qa_s.reshape(db, t_new, MOBA_HEADS, HEAD_DIM).transpose(0, 2, 1, 3)
    d_own = np.arange(t_new)[:, None] - np.arange(8)[None, :]
    d_own = np.where(np.arange(8)[None, :] < t_new, d_own, -1)
    bown = _bias_from_dist(tab_a, d_own)[..., None]
    d_near = (past + np.arange(8))[:, None] - (past - page + np.arange(page))[None, :]
    bnear = _bias_from_dist(tab_a, d_near)
    cfar_s = jnp.broadcast_to(far_a[:, None, None], (MOBA_HEADS, 1, page))
    o_heads = _moba_sample(page_table, sel, pad_to(q_heads, 2, 8), pad_to(ka_s, 2, 8), pad_to(va_s, 2, 8),
                           bown, bnear, cfar_s, pool_ka_t, pool_va_t)
    o_a_s = o_heads[:, :, :t_new].transpose(0, 2, 1, 3).reshape(n_s, w).astype(BF16)
    y_sample = _merge_ffn(xs, o_a_s, o_b_s, merge_weights, n_s).reshape(db, t_new, d)

    a_shape = (MOBA_HEADS, HEAD_DIM)
    b_shape = (DIFF_HEADS, 2 * HEAD_DIM)
    rows_first = lambda a: a.reshape((batch,) + a_shape + (seq,)).transpose(0, 3, 1, 2)[None]
    return (y_prompt, y_sample,
            rows_first(ka_t), rows_first(va_t),
            kb.reshape((1, batch, seq) + b_shape), vb.reshape((1, batch, seq) + b_shape),
            ka_s.transpose(0, 2, 1, 3)[None], va_s.transpose(0, 2, 1, 3)[None],
            kb_s.reshape((1, db, t_new) + b_shape), vb_s.reshape((1, db, t_new) + b_shape))
```

```python
import functools
import math

import numpy as np
import jax
import jax.numpy as jnp
from jax import lax
from jax.experimental import pallas as pl
from jax.experimental.pallas import tpu as pltpu

HEAD_DIM = 64
MOBA_HEADS = 8
DIFF_HEADS = 4
MOBA_BLOCK = 256
MOBA_TOPK = 3
REL_BUCKETS = 32
REL_MAX_DIST = 128
RMS_EPS = 1e-6
LAMBDA_INIT = 0.8 - 0.6 * math.exp(-0.3 * 0)

LANES = 128
ATT_WIDTH = 512
UNITS = ATT_WIDTH // LANES
TQ = MOBA_BLOCK
MERGE_TM = 512
PAGES_PER_STEP = 8
VMEM_LIMIT = 56 * 1024 * 1024
BF16_ROWS = 16
V_ROWS = LANES + BF16_ROWS
LOG2E = math.log2(math.e)

F32 = jnp.float32
BF16 = jnp.bfloat16
NEG_INF = float("-inf")


def _rel_bucket_np(dist):
    n = np.maximum(dist, 0)
    exact = REL_BUCKETS // 2
    nf = np.maximum(n, 1).astype(np.float64)
    large = exact + (np.log(nf / exact) / math.log(REL_MAX_DIST / exact)
                     * (REL_BUCKETS - exact)).astype(np.int64)
    return np.where(n < exact, n, np.minimum(large, REL_BUCKETS - 1)).astype(np.int32)


def _bias_from_dist(tab, dist, per_row=False):
    idx = jnp.asarray(_rel_bucket_np(dist))
    col = (lambda k: tab[:, k:k + 1]) if per_row else (
        lambda k: tab[:, k].reshape((-1,) + (1,) * dist.ndim))
    b = jnp.broadcast_to(col(0), dist.shape if per_row else (tab.shape[0],) + dist.shape)
    for k in range(1, REL_BUCKETS):
        b = jnp.where(idx == k, col(k), b)
    return jnp.where(jnp.asarray(dist >= 0), b, NEG_INF)


def _nt(a, b, **kw):
    return lax.dot_general(a, b, (((1,), (1,)), ((), ())), preferred_element_type=F32, **kw)


def _const_spec(shape):
    n = len(shape)
    return pl.BlockSpec(shape, lambda *_: (0,) * n, pipeline_mode=pl.Buffered(1))


def _with_ones_rows(vt):
    row = lax.broadcasted_iota(jnp.int32, (BF16_ROWS, vt.shape[1]), 0)
    ones = jnp.where(row == 0, 1.0, 0.0).astype(BF16)
    parts = []
    for u in range(UNITS):
        parts += [vt[u * LANES:(u + 1) * LANES].astype(BF16), ones]
    return jnp.concatenate(parts, axis=0)


def _proj_kernel(scale, x_ref, g_ref, w_ref, qk_ref, kat_ref, vat_ref, kb_ref, vb_ref, vat16_ref,
                 vbt16_ref, kmean_ref):
    x = x_ref[...]
    xn = x * lax.rsqrt(jnp.mean(x * x, axis=-1, keepdims=True) + RMS_EPS) * g_ref[...]
    p = jnp.dot(xn.astype(BF16), w_ref[...], preferred_element_type=F32)
    w = ATT_WIDTH
    ka, va = p[:, w:2 * w], p[:, 2 * w:3 * w]
    kb, vb = p[:, 4 * w:5 * w], p[:, 5 * w:6 * w]
    vat = va.T
    kat_ref[0] = ka.T
    vat_ref[0] = vat
    kb_ref[...] = kb
    vb_ref[...] = vb
    qk_ref[:, 0:w] = (p[:, 0:w] * scale).astype(BF16)
    qk_ref[:, w:2 * w] = ka.astype(BF16)
    qk_ref[:, 2 * w:3 * w] = (p[:, 3 * w:4 * w] * scale).astype(BF16)
    qk_ref[:, 3 * w:4 * w] = kb.astype(BF16)
    vat16_ref[0] = _with_ones_rows(vat)
    vbt16_ref[0] = _with_ones_rows(vb.T)
    kmean_ref[0] = jnp.mean(ka, axis=0, keepdims=True)


def _project(x2d, g, w_bf16, tm, tiles_per_seq, q_scale):
    n, d = x2d.shape
    w = ATT_WIDTH
    steps = n // tm
    seq = tm * tiles_per_seq
    row = lambda i: (i, 0)
    f32_rows = jax.ShapeDtypeStruct((n, w), F32)
    f32_t = jax.ShapeDtypeStruct((n // seq, w, seq), F32)
    t_spec = pl.BlockSpec((1, w, tm), lambda i: (i // tiles_per_seq, 0, i % tiles_per_seq))
    vt16 = jax.ShapeDtypeStruct((steps, UNITS * V_ROWS, tm), BF16)
    return pl.pallas_call(
        functools.partial(_proj_kernel, q_scale),
        grid=(steps,),
        in_specs=[pl.BlockSpec((tm, d), row), _const_spec((1, d)), _const_spec((d, 6 * w))],
        out_specs=[pl.BlockSpec((tm, 4 * w), row), t_spec, t_spec,
                   pl.BlockSpec((tm, w), row), pl.BlockSpec((tm, w), row)]
                  + [pl.BlockSpec((1, UNITS * V_ROWS, tm), lambda i: (i, 0, 0))] * 2
                  + [pl.BlockSpec((1, 1, w), lambda i: (i, 0, 0))],
        out_shape=[jax.ShapeDtypeStruct((n, 4 * w), BF16), f32_t, f32_t, f32_rows, f32_rows,
                   vt16, vt16, jax.ShapeDtypeStruct((steps, 1, w), F32)],
        compiler_params=pltpu.CompilerParams(dimension_semantics=("arbitrary",),
                                             vmem_limit_bytes=VMEM_LIMIT),
    )(x2d, g.reshape(1, d), w_bf16)


def _split_halves(q2):
    lane = lax.broadcasted_iota(jnp.int32, q2.shape, 1)
    zero = jnp.zeros_like(q2)
    return jnp.concatenate([jnp.where(lane < HEAD_DIM, q2, zero),
                            jnp.where(lane >= HEAD_DIM, q2, zero)], axis=0)


def _online_update(u, s, shift, vt, m_ref, acc_ref):
    m_old = m_ref[u]
    m_new = jnp.maximum(m_old, jnp.max(s, axis=0, keepdims=True) + shift)
    p = jnp.exp2(s - (m_new - shift))
    alpha = jnp.exp2(m_old - m_new)
    acc_ref[u] = alpha * acc_ref[u] + jnp.dot(vt, p.astype(BF16), preferred_element_type=F32)
    m_ref[u] = m_new


def _units_ahead(score_fn, fold_fn, first=None, after=None):
    s_next = score_fn(0) if first is None else first
    for u in range(UNITS):
        s = s_next
        if u + 1 < UNITS:
            s_next = score_fn(u + 1)
        elif after is not None:
            s_next = after()
        fold_fn(u, s)
    return s_next


def _attn_kernel(moba, *refs):
    if moba:
        (q_ref, k_ref, vt_ref, bias_ref, cfar_ref, kmean_ref, o_ref,
         qs_ref, m_ref, acc_ref, sh_ref) = refs
    else:
        (q_ref, k_ref, vt_ref, bias_ref, cfar_ref, lam_ref, subln_ref, o_ref,
         qs_ref, m_ref, acc_ref) = refs
    i = pl.program_id(1)
    tq = TQ

    def k_tile(n, u):
        return k_ref[pl.ds(pl.multiple_of(n * tq, tq), tq), u * LANES:(u + 1) * LANES]

    def vt_tile(n, u):
        return vt_ref[n, u * V_ROWS:(u + 1) * V_ROWS, :]

    def scores(n):
        return lambda u: _nt(k_tile(n, u), qs_ref[u])

    for u in range(UNITS):
        qs = _split_halves(q_ref[:, u * LANES:(u + 1) * LANES])
        qs_ref[u] = qs
        if moba:
            km = kmean_ref[0, :, u * LANES:(u + 1) * LANES]
            hi = km.astype(BF16)
            r1 = km - hi.astype(F32)
            mid = r1.astype(BF16)
            lo = (r1 - mid.astype(F32)).astype(BF16)
            g = _nt(hi, qs) + _nt(mid, qs) + _nt(lo, qs)
            nb = g.shape[0]
            blk = lax.broadcasted_iota(jnp.int32, g.shape, 0)
            valid = blk < i
            g = jnp.where(valid, g, NEG_INF)
            rank = jnp.zeros(g.shape, jnp.int32)
            for mm in range(nb):
                gm = g[mm:mm + 1, :]
                ahead = (gm > g) | ((gm == g) & (mm < blk))
                rank = rank + ahead.astype(jnp.int32)
            sel = valid & (rank < MOBA_TOPK)
            sh_ref[u] = jnp.where(sel, cfar_ref[u], NEG_INF)

    def fold_own(u, s):
        s = s + bias_ref[u, 0]
        m = jnp.max(s, axis=0, keepdims=True)
        p = jnp.exp2(s - m)
        m_ref[u] = m
        acc_ref[u] = jnp.dot(vt_tile(i, u), p.astype(BF16), preferred_element_type=F32)

    _units_ahead(scores(i), fold_own)

    @pl.when(i >= 1)
    def _():
        def fold_prev(u, s):
            if moba:
                shift = sh_ref[u, pl.ds(i - 1, 1), :] - cfar_ref[u]
            else:
                shift = jnp.zeros((1, 2 * tq), F32)
            _online_update(u, s + bias_ref[u, 1], shift, vt_tile(i - 1, u), m_ref, acc_ref)

        _units_ahead(scores(i - 1), fold_prev)

    def far(n, s_first):
        def fold_far(u, s):
            shift = sh_ref[u, pl.ds(n, 1), :] if moba else cfar_ref[u]
            _online_update(u, s, shift, vt_tile(n, u), m_ref, acc_ref)

        nxt = jnp.minimum(n + 1, i - 2)
        return _units_ahead(scores(n), fold_far, first=s_first, after=lambda: scores(nxt)(0))

    lax.fori_loop(0, i - 1, far, scores(0)(0))

    chan = lax.broadcasted_iota(jnp.int32, (LANES, tq), 0)
    if not moba:
        lp = lam_ref[...]
        lam = (jnp.exp(jnp.sum(lp[0:1] * lp[1:2], axis=-1, keepdims=True))
               - jnp.exp(jnp.sum(lp[2:3] * lp[3:4], axis=-1, keepdims=True)) + LAMBDA_INIT)
    for u in range(UNITS):
        acc = acc_ref[u]
        o = acc[:LANES] / acc[LANES:LANES + 1]
        if moba:
            out = jnp.where(chan < HEAD_DIM, o[:, :tq], o[:, tq:])
        else:
            out = o[:, :tq] - lam * o[:, tq:]
            out = out * lax.rsqrt(jnp.mean(out * out, axis=0, keepdims=True) + RMS_EPS)
            out = out * subln_ref[...] * (1.0 - LAMBDA_INIT)
        o_ref[:, u * LANES:(u + 1) * LANES] = out.T.astype(o_ref.dtype)


def _prompt_attention(moba, qk, vt, col0, batch, seq, bias, cfar, extra):
    nt = seq // TQ
    w = ATT_WIDTH
    in_specs = [
        pl.BlockSpec((TQ, w), lambda b, i: (b * nt + i, col0)),
        pl.BlockSpec((seq, w), lambda b, i: (b, col0 + 1)),
        pl.BlockSpec((nt, UNITS * V_ROWS, TQ), lambda b, i: (b, 0, 0)),
        _const_spec(bias.shape),
        _const_spec(cfar.shape),
    ]
    scratch = [pltpu.VMEM((UNITS, 2 * TQ, LANES), BF16),
               pltpu.VMEM((UNITS, 1, 2 * TQ), F32),
               pltpu.VMEM((UNITS, V_ROWS, 2 * TQ), F32)]
    if moba:
        (kmean,) = extra
        in_specs.append(pl.BlockSpec((1, nt, w), lambda b, i: (b, 0, 0)))
        scratch.append(pltpu.VMEM((UNITS, nt, 2 * TQ), F32))
    else:
        lam_params, subln_col = extra
        in_specs += [_const_spec(lam_params.shape), _const_spec(subln_col.shape)]
    return pl.pallas_call(
        functools.partial(_attn_kernel, moba),
        grid=(batch, nt),
        in_specs=in_specs,
        out_specs=pl.BlockSpec((TQ, w), lambda b, i: (b * nt + i, 0)),
        out_shape=jax.ShapeDtypeStruct((batch * seq, w), BF16),
        scratch_shapes=scratch,
        compiler_params=pltpu.CompilerParams(dimension_semantics=("arbitrary", "arbitrary"),
                                             vmem_limit_bytes=VMEM_LIMIT),
    )(qk, qk, vt, bias, cfar, *extra)


def _prompt_bias(tab, heads_of_unit):
    r = np.arange(TQ)[None, :] - np.arange(TQ)[:, None]
    own = _bias_from_dist(tab, r)
    prev = _bias_from_dist(tab, r + TQ)
    far = tab[:, REL_BUCKETS - 1]
    bias, cfar = [], []
    for h0, h1 in heads_of_unit:
        bias.append(jnp.stack([jnp.concatenate([own[h0], own[h1]], axis=1),
                               jnp.concatenate([prev[h0], prev[h1]], axis=1)]))
        cfar.append(jnp.concatenate([jnp.full((1, TQ), far[h0]), jnp.full((1, TQ), far[h1])], axis=1))
    return jnp.stack(bias) * LOG2E, jnp.stack(cfar) * LOG2E


def _merge_ffn_kernel(ffn_chunk, x_ref, oa_ref, ob_ref, gmix_ref, wg_ref, bg_ref, wa_ref, wb_ref,
                      wo_ref, gffn_ref, wup_ref, wdn_ref, gfin_ref, y_ref):
    x = x_ref[...]
    d = x.shape[1]
    xn = (x * lax.rsqrt(jnp.mean(x * x, axis=-1, keepdims=True) + RMS_EPS) * gmix_ref[...]).astype(BF16)
    gates = jax.nn.sigmoid(jnp.dot(xn, wg_ref[...], preferred_element_type=F32) + bg_ref[...])
    merged = (gates[:, :d] * jnp.dot(oa_ref[...], wa_ref[...], preferred_element_type=F32)
              + gates[:, d:] * jnp.dot(ob_ref[...], wb_ref[...], preferred_element_type=F32))
    h = x + jnp.dot(merged.astype(BF16), wo_ref[...], preferred_element_type=F32)
    hn = (h * lax.rsqrt(jnp.mean(h * h, axis=-1, keepdims=True) + RMS_EPS) * gffn_ref[...]).astype(BF16)
    hidden = wdn_ref.shape[0]
    out = h
    for c in range(hidden // ffn_chunk):
        lo = c * ffn_chunk
        gate = jnp.dot(hn, wup_ref[:, lo:lo + ffn_chunk], preferred_element_type=F32)
        up = jnp.dot(hn, wup_ref[:, hidden + lo:hidden + lo + ffn_chunk], preferred_element_type=F32)
        act = (gate * jax.nn.sigmoid(gate) * up).astype(BF16)
        out = out + jnp.dot(act, wdn_ref[lo:lo + ffn_chunk, :], preferred_element_type=F32)
    y = out * lax.rsqrt(jnp.mean(out * out, axis=-1, keepdims=True) + RMS_EPS) * gfin_ref[...]
    y_ref[...] = y


def _merge_ffn(x2d, oa, ob, weights, tm):
    n, d = x2d.shape
    row = lambda i: (i, 0)
    in_specs = [pl.BlockSpec((tm, d), row), pl.BlockSpec((tm, ATT_WIDTH), row),
                pl.BlockSpec((tm, ATT_WIDTH), row)] + [_const_spec(w.shape) for w in weights]
    hidden = weights[-2].shape[0]
    return pl.pallas_call(
        functools.partial(_merge_ffn_kernel, math.gcd(hidden, 2 * LANES)),
        grid=(n // tm,),
        in_specs=in_specs,
        out_specs=pl.BlockSpec((tm, d), row),
        out_shape=jax.ShapeDtypeStruct((n, d), F32),
        compiler_params=pltpu.CompilerParams(dimension_semantics=("arbitrary",),
                                             vmem_limit_bytes=VMEM_LIMIT),
    )(x2d, oa, ob, *weights)


def _top_lanes(g, count):
    lane = lax.broadcasted_iota(jnp.int32, g.shape, 1)
    out = jnp.zeros(g.shape, jnp.int32)
    for k in range(count):
        best = jnp.max(g, axis=-1, keepdims=True)
        idx = jnp.min(jnp.where(g == best, lane, LANES), axis=-1, keepdims=True)
        out = jnp.where(lane == k, idx, out)
        g = jnp.where(lane == idx, NEG_INF, g)
    return out


def _stream_kernel(ppb, pt_ref, qb_ref, qa_ref, knew_ref, vnew_ref, bnew_ref, bias_ref, lam_ref,
                   subln_ref, *refs):
    del pt_ref
    pps = PAGES_PER_STEP
    kb_refs, vb_refs, ka_refs = refs[:pps], refs[pps:2 * pps], refs[2 * pps:3 * pps]
    ob_ref, idx_ref, m_ref, l_ref, acc_ref, gate_ref = refs[3 * pps:]
    c = pl.program_id(1)
    nc = pl.num_programs(1)
    qb = qb_ref[0]
    half = qb.shape[0] // 2

    @pl.when(c == 0)
    def _():
        s = _nt(qb, knew_ref[0]) + bnew_ref[...]
        m = jnp.max(s, axis=-1, keepdims=True)
        p = jnp.exp(s - m)
        m_ref[...] = m
        l_ref[...] = jnp.sum(p, axis=-1, keepdims=True)
        acc_ref[...] = jnp.dot(p, vnew_ref[0], preferred_element_type=F32)
        gate_ref[...] = jnp.full(gate_ref.shape, NEG_INF, F32)

    s = jnp.concatenate([_nt(qb, kr[...]) for kr in kb_refs], axis=-1) + bias_ref[0]
    m_old = m_ref[...]
    m_new = jnp.maximum(m_old, jnp.max(s, axis=-1, keepdims=True))
    p = jnp.exp(s - m_new)
    alpha = jnp.exp(m_old - m_new)
    l_ref[...] = alpha * l_ref[...] + jnp.sum(p, axis=-1, keepdims=True)
    width = kb_refs[0].shape[0]
    pv = jnp.dot(p[:, :width], vb_refs[0][...], preferred_element_type=F32)
    for j in range(1, pps):
        pv = pv + jnp.dot(p[:, j * width:(j + 1) * width], vb_refs[j][...], preferred_element_type=F32)
    acc_ref[...] = alpha * acc_ref[...] + pv
    m_ref[...] = m_new

    qa = qa_ref[0]
    gate = gate_ref[...]
    lane = lax.broadcasted_iota(jnp.int32, gate.shape, 1)
    for j in range(pps // ppb):
        sa = jnp.dot(qa, ka_refs[j * ppb][...], preferred_element_type=F32)
        for jj in range(1, ppb):
            sa = sa + jnp.dot(qa, ka_refs[j * ppb + jj][...], preferred_element_type=F32)
        g = jnp.sum(sa, axis=-1, keepdims=True)
        gate = jnp.where(lane == c * (pps // ppb) + j, g, gate)
    gate_ref[...] = gate

    @pl.when(c == nc - 1)
    def _():
        lp = lam_ref[...]
        lam = (jnp.exp(jnp.sum(lp[0:1] * lp[1:2], axis=-1, keepdims=True))
               - jnp.exp(jnp.sum(lp[2:3] * lp[3:4], axis=-1, keepdims=True)) + LAMBDA_INIT)
        o = acc_ref[...] / l_ref[...]
        out = o[:half] - lam * o[half:]
        out = out * lax.rsqrt(jnp.mean(out * out, axis=-1, keepdims=True) + RMS_EPS)
        ob_ref[0] = out * subln_ref[...] * (1.0 - LAMBDA_INIT)
        idx_ref[0] = _top_lanes(gate, MOBA_TOPK)


def _sample_stream(page_table, qb_rows, qa_rows, knew, vnew, bnew, bias, lam_params, subln,
                   pool_kb, pool_vb, pool_ka_t, page):
    db, n_pages = page_table.shape
    pps = PAGES_PER_STEP
    nc = n_pages // pps
    rows_b, rows_a = qb_rows.shape[1], qa_rows.shape[1]
    blk_b, blk_a = pool_kb.shape[1:], pool_ka_t.shape[1:]

    def page_spec(j, blk):
        return pl.BlockSpec((None,) + blk, lambda b, c, pt: (pt[b * n_pages + c * pps + j], 0, 0))

    per_b = lambda b, c, pt: (b, 0, 0)
    fixed = lambda b, c, pt: (0, 0)
    in_specs = [
        pl.BlockSpec((1,) + qb_rows.shape[1:], per_b),
        pl.BlockSpec((1,) + qa_rows.shape[1:], per_b),
        pl.BlockSpec((1,) + knew.shape[1:], per_b),
        pl.BlockSpec((1,) + vnew.shape[1:], per_b),
        pl.BlockSpec(bnew.shape, fixed),
        pl.BlockSpec((1,) + bias.shape[1:], lambda b, c, pt: (jnp.where(c == nc - 1, 1, 0), 0, 0)),
        pl.BlockSpec(lam_params.shape, fixed),
        pl.BlockSpec(subln.shape, fixed),
    ] + [page_spec(j, blk_b) for j in range(pps)] * 2 + [page_spec(j, blk_a) for j in range(pps)]
    grid_spec = pltpu.PrefetchScalarGridSpec(
        num_scalar_prefetch=1,
        grid=(db, nc),
        in_specs=in_specs,
        out_specs=[pl.BlockSpec((1, rows_b // 2, LANES), per_b), pl.BlockSpec((1, rows_a, LANES), per_b)],
        scratch_shapes=[pltpu.VMEM((rows_b, 1), F32), pltpu.VMEM((rows_b, 1), F32),
                        pltpu.VMEM((rows_b, LANES), F32), pltpu.VMEM((rows_a, LANES), F32)],
    )
    return pl.pallas_call(
        functools.partial(_stream_kernel, MOBA_BLOCK // page),
        grid_spec=grid_spec,
        out_shape=[jax.ShapeDtypeStruct((db, rows_b // 2, LANES), F32),
                   jax.ShapeDtypeStruct((db, rows_a, LANES), jnp.int32)],
        compiler_params=pltpu.CompilerParams(dimension_semantics=("arbitrary", "arbitrary"),
                                             vmem_limit_bytes=VMEM_LIMIT),
    )(page_table.reshape(-1), qb_rows, qa_rows, knew, vnew, bnew, bias, lam_params, subln,
      *([pool_kb] * pps), *([pool_vb] * pps), *([pool_ka_t] * pps))


def _moba_sample_kernel(ppb, last_block, t_new, pt_ref, sel_ref, q_ref, knew_ref, vnew_ref, bown_ref,
                        bnear_ref, cfar_ref, *refs):
    del pt_ref
    n_sel = ppb * MOBA_TOPK
    n_pages = t_new * n_sel
    k_refs, v_refs = refs[:n_pages], refs[n_pages:2 * n_pages]
    o_ref = refs[2 * n_pages]
    b, h = pl.program_id(0), pl.program_id(1)
    n_heads = pl.num_programs(1)
    page = k_refs[0].shape[1]

    o_ref[...] = jnp.zeros_like(o_ref)
    for t in range(t_new):
        q1 = q_ref[0, 0, t:t + 1, :]
        q8 = jnp.broadcast_to(q1, (8, HEAD_DIM))
        s_own = jnp.sum(q1 * knew_ref[0, 0], axis=-1, keepdims=True) + bown_ref[0, t]
        scores = []
        for j in range(n_sel):
            bias = cfar_ref[0]
            if j % ppb == ppb - 1:
                blk = sel_ref[((b * n_heads + h) * t_new + t) * MOBA_TOPK + j // ppb]
                bias = jnp.where(blk == last_block, bnear_ref[0, t:t + 1, :], bias)
            scores.append(jnp.dot(q8, k_refs[t * n_sel + j][...], preferred_element_type=F32) + bias)
        s = jnp.concatenate(scores, axis=-1)
        m = jnp.maximum(jnp.max(s, axis=-1, keepdims=True)[0:1], jnp.max(s_own, axis=0, keepdims=True))
        p = jnp.exp(s - m)
        p_own = jnp.exp(s_own - m)
        l = jnp.sum(p, axis=-1, keepdims=True)[0:1] + jnp.sum(p_own, axis=0, keepdims=True)
        acc = jnp.sum(p_own * vnew_ref[0, 0], axis=0, keepdims=True)
        for j in range(n_sel):
            acc = acc + _nt(p[:, j * page:(j + 1) * page], v_refs[t * n_sel + j][...])[0:1]
        o_ref[0, 0, t:t + 1, :] = acc / l


def _moba_sample(page_table, sel, q8, knew8, vnew8, bown, bnear, cfar, pool_k_t, pool_v_t):
    db, n_pages = page_table.shape
    page = pool_k_t.shape[3]
    t_new = sel.shape[2]
    ppb = MOBA_BLOCK // page
    last_block = n_pages // ppb - 1
    n_sel = MOBA_TOPK * ppb

    def page_spec(jt):
        t, k, p = jt // n_sel, (jt % n_sel) // ppb, jt % ppb

        def index(b, h, pt, sl):
            blk = sl[((b * MOBA_HEADS + h) * t_new + t) * MOBA_TOPK + k]
            return (pt[b * n_pages + blk * ppb + p], h, 0, 0)
        return pl.BlockSpec((None, None, HEAD_DIM, page), index)

    per_bh = lambda b, h, pt, sl: (b, h, 0, 0)
    in_specs = [
        pl.BlockSpec((1, 1, 8, HEAD_DIM), per_bh),
        pl.BlockSpec((1, 1, 8, HEAD_DIM), per_bh),
        pl.BlockSpec((1, 1, 8, HEAD_DIM), per_bh),
        pl.BlockSpec((1, t_new, 8, 1), lambda b, h, pt, sl: (h, 0, 0, 0)),
        pl.BlockSpec((1, 8, page), lambda b, h, pt, sl: (h, 0, 0)),
        pl.BlockSpec((1, 1, page), lambda b, h, pt, sl: (h, 0, 0)),
    ] + [page_spec(jt) for jt in range(t_new * n_sel)] * 2
    grid_spec = pltpu.PrefetchScalarGridSpec(
        num_scalar_prefetch=2,
        grid=(db, MOBA_HEADS),
        in_specs=in_specs,
        out_specs=pl.BlockSpec((1, 1, 8, HEAD_DIM), per_bh),
    )
    return pl.pallas_call(
        functools.partial(_moba_sample_kernel, ppb, last_block, t_new),
        grid_spec=grid_spec,
        out_shape=jax.ShapeDtypeStruct((db, MOBA_HEADS, 8, HEAD_DIM), F32),
        compiler_params=pltpu.CompilerParams(dimension_semantics=("arbitrary", "arbitrary")),
    )(page_table.reshape(-1), sel.reshape(-1), q8, knew8, vnew8, bown, bnear, cfar,
      *([pool_k_t] * (t_new * n_sel)), *([pool_v_t] * (t_new * n_sel)))


def kernel(x_prompt, x_sample, cache_k_a, cache_v_a, cache_k_b, cache_v_b, page_table, g_mix, w_in,
           w_merge_gate, b_merge_gate, w_branch_a, w_branch_b, w_out, lambda_q1, lambda_k1, lambda_q2,
           lambda_k2, subln_w, g_ffn, w_ffn_up, w_ffn_down, rel_bias, g_final):
    batch, seq, d = x_prompt.shape
    db, t_new, _ = x_sample.shape
    depth, n_pool, page = cache_k_a.shape[:3]
    n_pages = page_table.shape[1]
    past = n_pages * page
    nb_past = past // MOBA_BLOCK
    w = ATT_WIDTH
    assert depth == 1 and seq % TQ == 0 and past % MOBA_BLOCK == 0 and MOBA_BLOCK % page == 0
    assert n_pages % PAGES_PER_STEP == 0 and MOBA_TOPK <= nb_past <= LANES and t_new <= 8
    assert page == LANES and PAGES_PER_STEP % (MOBA_BLOCK // page) == 0

    tab = rel_bias.astype(F32).T
    tab_a, tab_b = tab[:MOBA_HEADS], tab[MOBA_HEADS:]
    far_a, far_b = tab_a[:, REL_BUCKETS - 1], tab_b[:, REL_BUCKETS - 1]
    lam_params = jnp.concatenate([lambda_q1, lambda_k1, lambda_q2, lambda_k2], axis=0).astype(F32)
    subln = subln_w.astype(F32).reshape(1, 2 * HEAD_DIM)
    w_in_b = w_in[0].astype(BF16)
    merge_weights = (g_mix.reshape(1, d), w_merge_gate[0].astype(BF16), b_merge_gate.reshape(1, 2 * d),
                     w_branch_a[0].astype(BF16), w_branch_b[0].astype(BF16), w_out[0].astype(BF16),
                     g_ffn.reshape(1, d), w_ffn_up[0].astype(BF16), w_ffn_down[0].astype(BF16),
                     g_final.reshape(1, d))

    xp = x_prompt.reshape(batch * seq, d)
    scale = HEAD_DIM ** -0.5
    qk, ka_t, va_t, kb, vb, vat, vbt, kmean = _project(xp, g_mix, w_in_b, TQ, seq // TQ, scale * LOG2E)
    bias_a, cfar_a = _prompt_bias(tab_a, [(2 * u, 2 * u + 1) for u in range(UNITS)])
    bias_b, cfar_b = _prompt_bias(tab_b, [(u, u) for u in range(UNITS)])
    o_a = _prompt_attention(True, qk, vat, 0, batch, seq, bias_a, cfar_a,
                            (kmean.reshape(batch, seq // TQ, w),))
    o_b = _prompt_attention(False, qk, vbt, 2, batch, seq, bias_b, cfar_b,
                            (lam_params, subln.reshape(2 * HEAD_DIM, 1)))
    y_prompt = _merge_ffn(xp, o_a, o_b, merge_weights, MERGE_TM).reshape(batch, seq, d)

    n_s = db * t_new
    xs = x_sample.reshape(n_s, d)
    qk_s, ka_st, va_st, kb_s, vb_s, _, _, _ = _project(xs, g_mix, w_in_b, n_s, 1, scale)
    heads_first = lambda a: a.reshape(MOBA_HEADS, HEAD_DIM, db, t_new).transpose(2, 0, 3, 1)
    ka_s, va_s = heads_first(ka_st), heads_first(va_st)
    qk_s3 = qk_s.reshape(db, t_new, 4 * w).astype(F32)
    pad_to = lambda a, axis, n: jnp.pad(a, [(0, n - a.shape[i] if i == axis else 0) for i in range(a.ndim)])
    hb = DIFF_HEADS
    pool_kb = cache_k_b.reshape(n_pool, page * hb, 2 * HEAD_DIM)
    pool_vb = cache_v_b.reshape(n_pool, page * hb, 2 * HEAD_DIM)
    pool_ka_t = jnp.transpose(cache_k_a[0], (0, 2, 3, 1))
    pool_va_t = jnp.transpose(cache_v_a[0], (0, 2, 3, 1))

    lane = np.arange(LANES)
    qb16 = qk_s3[:, :, 2 * w:3 * w].reshape(db, t_new, hb, LANES).transpose(0, 2, 1, 3)
    qb16 = qb16.reshape(db, hb * t_new, LANES)
    qb_rows = jnp.concatenate([jnp.where(lane < HEAD_DIM, qb16, 0.0),
                               jnp.where(lane >= HEAD_DIM, qb16, 0.0)], axis=1)
    rows_b = 2 * hb * t_new
    head_of_row = (np.arange(rows_b) // t_new) % hb
    tok_of_row = np.arange(rows_b) % t_new
    head_mask = lambda cols: jnp.asarray(head_of_row[:, None] == (cols % hb)[None, :])
    tab_rows = tab_b[head_of_row]
    cols_new = np.arange(LANES)
    d_new = np.where(cols_new[None, :] < t_new * hb, tok_of_row[:, None] - (cols_new // hb)[None, :], -1)
    bnew = jnp.where(head_mask(cols_new), _bias_from_dist(tab_rows, d_new, per_row=True), NEG_INF)
    span = PAGES_PER_STEP * page
    cols = np.arange(span * hb)
    d_last = (past + tok_of_row)[:, None] - (past - span + cols // hb)[None, :]
    bias_s = jnp.stack([jnp.broadcast_to(far_b[head_of_row][:, None], (rows_b, span * hb)),
                        _bias_from_dist(tab_rows, d_last, per_row=True)])
    bias_s = jnp.where(head_mask(cols)[None], bias_s, NEG_INF)
    seg = np.arange(w) // HEAD_DIM
    qa_s = qk_s3[:, :, 0:w]
    qa_rows = jnp.where(jnp.asarray(seg[None, None, :] == np.arange(MOBA_HEADS)[:, None, None]),
                        qa_s[:, None], 0.0).reshape(db, MOBA_HEADS * t_new, w)
    knew_b = pad_to(kb_s.reshape(db, t_new * hb, 2 * HEAD_DIM), 1, LANES)
    vnew_b = pad_to(vb_s.reshape(db, t_new * hb, 2 * HEAD_DIM), 1, LANES)
    o_b16, sel_rows = _sample_stream(
        page_table, qb_rows, qa_rows, knew_b, vnew_b, bnew, bias_s, lam_params, subln,
        pool_kb, pool_vb, pool_ka_t.reshape(n_pool, w, page), page)
    o_b_s = o_b16.reshape(db, hb, t_new, LANES).transpose(0, 2, 1, 3).reshape(n_s, w).astype(BF16)
    sel = sel_rows[:, :, :MOBA_TOPK].reshape(db, MOBA_HEADS, t_new, MOBA_TOPK)

    q_heads = qa_s.reshape(db, t_new, MOBA_HEADS, HEAD_DIM).transpose(0, 2, 1, 3)
    d_own = np.arange(t_new)[:, None] - np.arange(8)[None, :]
    d_own = np.where(np.arange(8)[None, :] < t_new, d_own, -1)
    bown = _bias_from_dist(tab_a, d_own)[..., None]
    d_near = (past + np.arange(8))[:, None] - (past - page + np.arange(page))[None, :]
    bnear = _bias_from_dist(tab_a, d_near)
    cfar_s = jnp.broadcast_to(far_a[:, None, None], (MOBA_HEADS, 1, page))
    o_heads = _moba_sample(page_table, sel, pad_to(q_heads, 2, 8), pad_to(ka_s, 2, 8), pad_to(va_s, 2, 8),
                           bown, bnear, cfar_s, pool_ka_t, pool_va_t)
    o_a_s = o_heads[:, :, :t_new].transpose(0, 2, 1, 3).reshape(n_s, w).astype(BF16)
    y_sample = _merge_ffn(xs, o_a_s, o_b_s, merge_weights, n_s).reshape(db, t_new, d)

    a_shape = (MOBA_HEADS, HEAD_DIM)
    b_shape = (DIFF_HEADS, 2 * HEAD_DIM)
    rows_first = lambda a: a.reshape((batch,) + a_shape + (seq,)).transpose(0, 3, 1, 2)[None]
    return (y_prompt, y_sample,
            rows_first(ka_t), rows_first(va_t),
            kb.reshape((1, batch, seq) + b_shape), vb.reshape((1, batch, seq) + b_shape),
            ka_s.transpose(0, 2, 1, 3)[None], va_s.transpose(0, 2, 1, 3)[None],
            kb_s.reshape((1, db, t_new) + b_shape), vb_s.reshape((1, db, t_new) + b_shape))
```

```python
import functools
import math

import numpy as np
import jax
import jax.numpy as jnp
from jax import lax
from jax.experimental import pallas as pl
from jax.experimental.pallas import tpu as pltpu

HEAD_DIM = 64
MOBA_HEADS = 8
DIFF_HEADS = 4
MOBA_BLOCK = 256
MOBA_TOPK = 3
REL_BUCKETS = 32
REL_MAX_DIST = 128
RMS_EPS = 1e-6
LAMBDA_INIT = 0.8 - 0.6 * math.exp(-0.3 * 0)

LANES = 128
ATT_WIDTH = 512
UNITS = ATT_WIDTH // LANES
TQ = MOBA_BLOCK
MERGE_TM = 512
PAGES_PER_STEP = 8
VMEM_LIMIT = 56 * 1024 * 1024
BF16_ROWS = 16
V_ROWS = LANES + BF16_ROWS
LOG2E = math.log2(math.e)

F32 = jnp.float32
BF16 = jnp.bfloat16
NEG_INF = float("-inf")


def _rel_bucket_np(dist):
    n = np.maximum(dist, 0)
    exact = REL_BUCKETS // 2
    nf = np.maximum(n, 1).astype(np.float64)
    large = exact + (np.log(nf / exact) / math.log(REL_MAX_DIST / exact)
                     * (REL_BUCKETS - exact)).astype(np.int64)
    return np.where(n < exact, n, np.minimum(large, REL_BUCKETS - 1)).astype(np.int32)


def _bias_from_dist(tab, dist, per_row=False):
    idx = jnp.asarray(_rel_bucket_np(dist))
    col = (lambda k: tab[:, k:k + 1]) if per_row else (
        lambda k: tab[:, k].reshape((-1,) + (1,) * dist.ndim))
    b = jnp.broadcast_to(col(0), dist.shape if per_row else (tab.shape[0],) + dist.shape)
    for k in range(1, REL_BUCKETS):
        b = jnp.where(idx == k, col(k), b)
    return jnp.where(jnp.asarray(dist >= 0), b, NEG_INF)


def _nt(a, b, **kw):
    return lax.dot_general(a, b, (((1,), (1,)), ((), ())), preferred_element_type=F32, **kw)


def _const_spec(shape):
    n = len(shape)
    return pl.BlockSpec(shape, lambda *_: (0,) * n, pipeline_mode=pl.Buffered(1))


def _with_ones_rows(vt):
    row = lax.broadcasted_iota(jnp.int32, (BF16_ROWS, vt.shape[1]), 0)
    ones = jnp.where(row == 0, 1.0, 0.0).astype(BF16)
    parts = []
    for u in range(UNITS):
        parts += [vt[u * LANES:(u + 1) * LANES].astype(BF16), ones]
    return jnp.concatenate(parts, axis=0)


def _proj_kernel(scale, x_ref, g_ref, w_ref, qk_ref, kat_ref, vat_ref, kb_ref, vb_ref, vat16_ref,
                 vbt16_ref, kmean_ref):
    x = x_ref[...]
    xn = x * lax.rsqrt(jnp.mean(x * x, axis=-1, keepdims=True) + RMS_EPS) * g_ref[...]
    p = jnp.dot(xn.astype(BF16), w_ref[...], preferred_element_type=F32)
    w = ATT_WIDTH
    ka, va = p[:, w:2 * w], p[:, 2 * w:3 * w]
    kb, vb = p[:, 4 * w:5 * w], p[:, 5 * w:6 * w]
    vat = va.T
    kat_ref[0] = ka.T
    vat_ref[0] = vat
    kb_ref[...] = kb
    vb_ref[...] = vb
    qk_ref[:, 0:w] = (p[:, 0:w] * scale).astype(BF16)
    qk_ref[:, w:2 * w] = ka.astype(BF16)
    qk_ref[:, 2 * w:3 * w] = (p[:, 3 * w:4 * w] * scale).astype(BF16)
    qk_ref[:, 3 * w:4 * w] = kb.astype(BF16)
    vat16_ref[0] = _with_ones_rows(vat)
    vbt16_ref[0] = _with_ones_rows(vb.T)
    kmean_ref[0] = jnp.mean(ka, axis=0, keepdims=True)


def _project(x2d, g, w_bf16, tm, tiles_per_seq, q_scale):
    n, d = x2d.shape
    w = ATT_WIDTH
    steps = n // tm
    seq = tm * tiles_per_seq
    row = lambda i: (i, 0)
    f32_rows = jax.ShapeDtypeStruct((n, w), F32)
    f32_t = jax.ShapeDtypeStruct((n // seq, w, seq), F32)
    t_spec = pl.BlockSpec((1, w, tm), lambda i: (i // tiles_per_seq, 0, i % tiles_per_seq))
    vt16 = jax.ShapeDtypeStruct((steps, UNITS * V_ROWS, tm), BF16)
    return pl.pallas_call(
        functools.partial(_proj_kernel, q_scale),
        grid=(steps,),
        in_specs=[pl.BlockSpec((tm, d), row), _const_spec((1, d)), _const_spec((d, 6 * w))],
        out_specs=[pl.BlockSpec((tm, 4 * w), row), t_spec, t_spec,
                   pl.BlockSpec((tm, w), row), pl.BlockSpec((tm, w), row)]
                  + [pl.BlockSpec((1, UNITS * V_ROWS, tm), lambda i: (i, 0, 0))] * 2
                  + [pl.BlockSpec((1, 1, w), lambda i: (i, 0, 0))],
        out_shape=[jax.ShapeDtypeStruct((n, 4 * w), BF16), f32_t, f32_t, f32_rows, f32_rows,
                   vt16, vt16, jax.ShapeDtypeStruct((steps, 1, w), F32)],
        compiler_params=pltpu.CompilerParams(dimension_semantics=("arbitrary",),
                                             vmem_limit_bytes=VMEM_LIMIT),
    )(x2d, g.reshape(1, d), w_bf16)


def _split_halves(q2):
    lane = lax.broadcasted_iota(jnp.int32, q2.shape, 1)
    zero = jnp.zeros_like(q2)
    return jnp.concatenate([jnp.where(lane < HEAD_DIM, q2, zero),
                            jnp.where(lane >= HEAD_DIM, q2, zero)], axis=0)


def _online_update(u, s, shift, vt, m_ref, acc_ref):
    m_old = m_ref[u]
    m_new = jnp.maximum(m_old, jnp.max(s, axis=0, keepdims=True) + shift)
    p = jnp.exp2(s - (m_new - shift))
    alpha = jnp.exp2(m_old - m_new)
    acc_ref[u] = alpha * acc_ref[u] + jnp.dot(vt, p.astype(BF16), preferred_element_type=F32)
    m_ref[u] = m_new


def _units_ahead(score_fn, fold_fn, first=None, after=None):
    s_next = score_fn(0) if first is None else first
    for u in range(UNITS):
        s = s_next
        if u + 1 < UNITS:
            s_next = score_fn(u + 1)
        elif after is not None:
            s_next = after()
        fold_fn(u, s)
    return s_next


def _attn_kernel(moba, *refs):
    if moba:
        (q_ref, k_ref, vt_ref, bias_ref, cfar_ref, kmean_ref, o_ref,
         qs_ref, m_ref, acc_ref, sh_ref) = refs
    else:
        (q_ref, k_ref, vt_ref, bias_ref, cfar_ref, lam_ref, subln_ref, o_ref,
         qs_ref, m_ref, acc_ref) = refs
    i = pl.program_id(1)
    tq = TQ

    def k_tile(n, u):
        return k_ref[pl.ds(pl.multiple_of(n * tq, tq), tq), u * LANES:(u + 1) * LANES]

    def vt_tile(n, u):
        return vt_ref[n, u * V_ROWS:(u + 1) * V_ROWS, :]

    def scores(n):
        return lambda u: _nt(k_tile(n, u), qs_ref[u])

    for u in range(UNITS):
        qs = _split_halves(q_ref[:, u * LANES:(u + 1) * LANES])
        qs_ref[u] = qs
        if moba:
            km = kmean_ref[0, :, u * LANES:(u + 1) * LANES]
            hi = km.astype(BF16)
            r1 = km - hi.astype(F32)
            mid = r1.astype(BF16)
            lo = (r1 - mid.astype(F32)).astype(BF16)
            g = _nt(hi, qs) + _nt(mid, qs) + _nt(lo, qs)
            nb = g.shape[0]
            blk = lax.broadcasted_iota(jnp.int32, g.shape, 0)
            valid = blk < i
            g = jnp.where(valid, g, NEG_INF)
            rank = jnp.zeros(g.shape, jnp.int32)
            for mm in range(nb):
                gm = g[mm:mm + 1, :]
                ahead = (gm > g) | ((gm == g) & (mm < blk))
                rank = rank + ahead.astype(jnp.int32)
            sel = valid & (rank < MOBA_TOPK)
            sh_ref[u] = jnp.where(sel, cfar_ref[u], NEG_INF)

    def fold_own(u, s):
        s = s + bias_ref[u, 0]
        m = jnp.max(s, axis=0, keepdims=True)
        p = jnp.exp2(s - m)
        m_ref[u] = m
        acc_ref[u] = jnp.dot(vt_tile(i, u), p.astype(BF16), preferred_element_type=F32)

    _units_ahead(scores(i), fold_own)

    @pl.when(i >= 1)
    def _():
        def fold_prev(u, s):
            if moba:
                shift = sh_ref[u, pl.ds(i - 1, 1), :] - cfar_ref[u]
            else:
                shift = jnp.zeros((1, 2 * tq), F32)
            _online_update(u, s + bias_ref[u, 1], shift, vt_tile(i - 1, u), m_ref, acc_ref)

        _units_ahead(scores(i - 1), fold_prev)

    def far(n, s_first):
        def fold_far(u, s):
            shift = sh_ref[u, pl.ds(n, 1), :] if moba else cfar_ref[u]
            _online_update(u, s, shift, vt_tile(n, u), m_ref, acc_ref)

        nxt = jnp.minimum(n + 1, i - 2)
        return _units_ahead(scores(n), fold_far, first=s_first, after=lambda: scores(nxt)(0))

    lax.fori_loop(0, i - 1, far, scores(0)(0))

    chan = lax.broadcasted_iota(jnp.int32, (LANES, tq), 0)
    if not moba:
        lp = lam_ref[...]
        lam = (jnp.exp(jnp.sum(lp[0:1] * lp[1:2], axis=-1, keepdims=True))
               - jnp.exp(jnp.sum(lp[2:3] * lp[3:4], axis=-1, keepdims=True)) + LAMBDA_INIT)
    for u in range(UNITS):
        acc = acc_ref[u]
        o = acc[:LANES] / acc[LANES:LANES + 1]
        if moba:
            out = jnp.where(chan < HEAD_DIM, o[:, :tq], o[:, tq:])
        else:
            out = o[:, :tq] - lam * o[:, tq:]
            out = out * lax.rsqrt(jnp.mean(out * out, axis=0, keepdims=True) + RMS_EPS)
            out = out * subln_ref[...] * (1.0 - LAMBDA_INIT)
        o_ref[:, u * LANES:(u + 1) * LANES] = out.T.astype(o_ref.dtype)


def _prompt_attention(moba, qk, vt, col0, batch, seq, bias, cfar, extra):
    nt = seq // TQ
    w = ATT_WIDTH
    in_specs = [
        pl.BlockSpec((TQ, w), lambda b, i: (b * nt + i, col0)),
        pl.BlockSpec((seq, w), lambda b, i: (b, col0 + 1)),
        pl.BlockSpec((nt, UNITS * V_ROWS, TQ), lambda b, i: (b, 0, 0)),
        _const_spec(bias.shape),
        _const_spec(cfar.shape),
    ]
    scratch = [pltpu.VMEM((UNITS, 2 * TQ, LANES), BF16),
               pltpu.VMEM((UNITS, 1, 2 * TQ), F32),
               pltpu.VMEM((UNITS, V_ROWS, 2 * TQ), F32)]
    if moba:
        (kmean,) = extra
        in_specs.append(pl.BlockSpec((1, nt, w), lambda b, i: (b, 0, 0)))
        scratch.append(pltpu.VMEM((UNITS, nt, 2 * TQ), F32))
    else:
        lam_params, subln_col = extra
        in_specs += [_const_spec(lam_params.shape), _const_spec(subln_col.shape)]
    return pl.pallas_call(
        functools.partial(_attn_kernel, moba),
        grid=(batch, nt),
        in_specs=in_specs,
        out_specs=pl.BlockSpec((TQ, w), lambda b, i: (b * nt + i, 0)),
        out_shape=jax.ShapeDtypeStruct((batch * seq, w), BF16),
        scratch_shapes=scratch,
        compiler_params=pltpu.CompilerParams(dimension_semantics=("arbitrary", "arbitrary"),
                                             vmem_limit_bytes=VMEM_LIMIT),
    )(qk, qk, vt, bias, cfar, *extra)


def _prompt_bias(tab, heads_of_unit):
    r = np.arange(TQ)[None, :] - np.arange(TQ)[:, None]
    own = _bias_from_dist(tab, r)
    prev = _bias_from_dist(tab, r + TQ)
    far = tab[:, REL_BUCKETS - 1]
    bias, cfar = [], []
    for h0, h1 in heads_of_unit:
        bias.append(jnp.stack([jnp.concatenate([own[h0], own[h1]], axis=1),
                               jnp.concatenate([prev[h0], prev[h1]], axis=1)]))
        cfar.append(jnp.concatenate([jnp.full((1, TQ), far[h0]), jnp.full((1, TQ), far[h1])], axis=1))
    return jnp.stack(bias) * LOG2E, jnp.stack(cfar) * LOG2E


def _merge_ffn_kernel(ffn_chunk, x_ref, oa_ref, ob_ref, gmix_ref, wg_ref, bg_ref, wa_ref, wb_ref,
                      wo_ref, gffn_ref, wup_ref, wdn_ref, gfin_ref, y_ref):
    x = x_ref[...]
    d = x.shape[1]
    xn = (x * lax.rsqrt(jnp.mean(x * x, axis=-1, keepdims=True) + RMS_EPS) * gmix_ref[...]).astype(BF16)
    gates = jax.nn.sigmoid(jnp.dot(xn, wg_ref[...], preferred_element_type=F32) + bg_ref[...])
    merged = (gates[:, :d] * jnp.dot(oa_ref[...], wa_ref[...], preferred_element_type=F32)
              + gates[:, d:] * jnp.dot(ob_ref[...], wb_ref[...], preferred_element_type=F32))
    h = x + jnp.dot(merged.astype(BF16), wo_ref[...], preferred_element_type=F32)
    hn = (h * lax.rsqrt(jnp.mean(h * h, axis=-1, keepdims=True) + RMS_EPS) * gffn_ref[...]).astype(BF16)
    hidden = wdn_ref.shape[0]
    out = h
    for c in range(hidden // ffn_chunk):
        lo = c * ffn_chunk
        gate = jnp.dot(hn, wup_ref[:, lo:lo + ffn_chunk], preferred_element_type=F32)
        up = jnp.dot(hn, wup_ref[:, hidden + lo:hidden + lo + ffn_chunk], preferred_element_type=F32)
        act = (gate * jax.nn.sigmoid(gate) * up).astype(BF16)
        out = out + jnp.dot(act, wdn_ref[lo:lo + ffn_chunk, :], preferred_element_type=F32)
    y = out * lax.rsqrt(jnp.mean(out * out, axis=-1, keepdims=True) + RMS_EPS) * gfin_ref[...]
    y_ref[...] = y


def _merge_ffn(x2d, oa, ob, weights, tm):
    n, d = x2d.shape
    row = lambda i: (i, 0)
    in_specs = [pl.BlockSpec((tm, d), row), pl.BlockSpec((tm, ATT_WIDTH), row),
                pl.BlockSpec((tm, ATT_WIDTH), row)] + [_const_spec(w.shape) for w in weights]
    hidden = weights[-2].shape[0]
    return pl.pallas_call(
        functools.partial(_merge_ffn_kernel, math.gcd(hidden, 2 * LANES)),
        grid=(n // tm,),
        in_specs=in_specs,
        out_specs=pl.BlockSpec((tm, d), row),
        out_shape=jax.ShapeDtypeStruct((n, d), F32),
        compiler_params=pltpu.CompilerParams(dimension_semantics=("arbitrary",),
                                             vmem_limit_bytes=VMEM_LIMIT),
    )(x2d, oa, ob, *weights)


def _top_lanes(g, count):
    lane = lax.broadcasted_iota(jnp.int32, g.shape, 1)
    out = jnp.zeros(g.shape, jnp.int32)
    for k in range(count):
        best = jnp.max(g, axis=-1, keepdims=True)
        idx = jnp.min(jnp.where(g == best, lane, LANES), axis=-1, keepdims=True)
        out = jnp.where(lane == k, idx, out)
        g = jnp.where(lane == idx, NEG_INF, g)
    return out


def _stream_kernel(ppb, pt_ref, qb_ref, qa_ref, knew_ref, vnew_ref, bnew_ref, bias_ref, lam_ref,
                   subln_ref, *refs):
    del pt_ref
    pps = PAGES_PER_STEP
    kb_refs, vb_refs, ka_refs = refs[:pps], refs[pps:2 * pps], refs[2 * pps:3 * pps]
    ob_ref, idx_ref, m_ref, l_ref, acc_ref, gate_ref = refs[3 * pps:]
    c = pl.program_id(1)
    nc = pl.num_programs(1)
    qb = qb_ref[0]
    half = qb.shape[0] // 2

    @pl.when(c == 0)
    def _():
        s = _nt(qb, knew_ref[0]) + bnew_ref[...]
        m = jnp.max(s, axis=-1, keepdims=True)
        p = jnp.exp(s - m)
        m_ref[...] = m
        l_ref[...] = jnp.sum(p, axis=-1, keepdims=True)
        acc_ref[...] = jnp.dot(p, vnew_ref[0], preferred_element_type=F32)
        gate_ref[...] = jnp.full(gate_ref.shape, NEG_INF, F32)

    s = jnp.concatenate([_nt(qb, kr[...]) for kr in kb_refs], axis=-1) + bias_ref[0]
    m_old = m_ref[...]
    m_new = jnp.maximum(m_old, jnp.max(s, axis=-1, keepdims=True))
    p = jnp.exp(s - m_new)
    alpha = jnp.exp(m_old - m_new)
    l_ref[...] = alpha * l_ref[...] + jnp.sum(p, axis=-1, keepdims=True)
    width = kb_refs[0].shape[0]
    pv = jnp.dot(p[:, :width], vb_refs[0][...], preferred_element_type=F32)
    for j in range(1, pps):
        pv = pv + jnp.dot(p[:, j * width:(j + 1) * width], vb_refs[j][...], preferred_element_type=F32)
    acc_ref[...] = alpha * acc_ref[...] + pv
    m_ref[...] = m_new

    qa = qa_ref[0]
    gate = gate_ref[...]
    lane = lax.broadcasted_iota(jnp.int32, gate.shape, 1)
    for j in range(pps // ppb):
        sa = jnp.dot(qa, ka_refs[j * ppb][...], preferred_element_type=F32)
        for jj in range(1, ppb):
            sa = sa + jnp.dot(qa, ka_refs[j * ppb + jj][...], preferred_element_type=F32)
        g = jnp.sum(sa, axis=-1, keepdims=True)
        gate = jnp.where(lane == c * (pps // ppb) + j, g, gate)
    gate_ref[...] = gate

    @pl.when(c == nc - 1)
    def _():
        lp = lam_ref[...]
        lam = (jnp.exp(jnp.sum(lp[0:1] * lp[1:2], axis=-1, keepdims=True))
               - jnp.exp(jnp.sum(lp[2:3] * lp[3:4], axis=-1, keepdims=True)) + LAMBDA_INIT)
        o = acc_ref[...] / l_ref[...]
        out = o[:half] - lam * o[half:]
        out = out * lax.rsqrt(jnp.mean(out * out, axis=-1, keepdims=True) + RMS_EPS)
        ob_ref[0] = out * subln_ref[...] * (1.0 - LAMBDA_INIT)
        idx_ref[0] = _top_lanes(gate, MOBA_TOPK)


def _sample_stream(page_table, qb_rows, qa_rows, knew, vnew, bnew, bias, lam_params, subln,
                   pool_kb, pool_vb, pool_ka_t, page):
    db, n_pages = page_table.shape
    pps = PAGES_PER_STEP
    nc = n_pages // pps
    rows_b, rows_a = qb_rows.shape[1], qa_rows.shape[1]
    blk_b, blk_a = pool_kb.shape[1:], pool_ka_t.shape[1:]

    def page_spec(j, blk):
        return pl.BlockSpec((None,) + blk, lambda b, c, pt: (pt[b * n_pages + c * pps + j], 0, 0))

    per_b = lambda b, c, pt: (b, 0, 0)
    fixed = lambda b, c, pt: (0, 0)
    in_specs = [
        pl.BlockSpec((1,) + qb_rows.shape[1:], per_b),
        pl.BlockSpec((1,) + qa_rows.shape[1:], per_b),
        pl.BlockSpec((1,) + knew.shape[1:], per_b),
        pl.BlockSpec((1,) + vnew.shape[1:], per_b),
        pl.BlockSpec(bnew.shape, fixed),
        pl.BlockSpec((1,) + bias.shape[1:], lambda b, c, pt: (jnp.where(c == nc - 1, 1, 0), 0, 0)),
        pl.BlockSpec(lam_params.shape, fixed),
        pl.BlockSpec(subln.shape, fixed),
    ] + [page_spec(j, blk_b) for j in range(pps)] * 2 + [page_spec(j, blk_a) for j in range(pps)]
    grid_spec = pltpu.PrefetchScalarGridSpec(
        num_scalar_prefetch=1,
        grid=(db, nc),
        in_specs=in_specs,
        out_specs=[pl.BlockSpec((1, rows_b // 2, LANES), per_b), pl.BlockSpec((1, rows_a, LANES), per_b)],
        scratch_shapes=[pltpu.VMEM((rows_b, 1), F32), pltpu.VMEM((rows_b, 1), F32),
                        pltpu.VMEM((rows_b, LANES), F32), pltpu.VMEM((rows_a, LANES), F32)],
    )
    return pl.pallas_call(
        functools.partial(_stream_kernel, MOBA_BLOCK // page),
        grid_spec=grid_spec,
        out_shape=[jax.ShapeDtypeStruct((db, rows_b // 2, LANES), F32),
                   jax.ShapeDtypeStruct((db, rows_a, LANES), jnp.int32)],
        compiler_params=pltpu.CompilerParams(dimension_semantics=("arbitrary", "arbitrary"),
                                             vmem_limit_bytes=VMEM_LIMIT),
    )(page_table.reshape(-1), qb_rows, qa_rows, knew, vnew, bnew, bias, lam_params, subln,
      *([pool_kb] * pps), *([pool_vb] * pps), *([pool_ka_t] * pps))


def _moba_sample_kernel(ppb, last_block, t_new, pages_per_req, pt_ref, sel_ref, q_ref, knew_ref,
                        vnew_ref, bown_ref, bnear_ref, cfar_ref, pool_k_ref, pool_v_ref, o_ref,
                        kbuf, vbuf, sem):
    n_sel = ppb * MOBA_TOPK
    n_cp = t_new * n_sel
    n_heads = q_ref.shape[1]
    n_it = q_ref.shape[0] * n_heads
    page = kbuf.shape[3]

    def page_copy(pool_ref, buf, kind, slot, jt, phys, h):
        return pltpu.make_async_copy(pool_ref.at[phys, h], buf.at[slot, jt], sem.at[slot, kind, jt])

    def start_all(it, slot):
        b, h = it // n_heads, it % n_heads
        for jt in range(n_cp):
            t, k, p = jt // n_sel, (jt % n_sel) // ppb, jt % ppb
            blk = sel_ref[(it * t_new + t) * MOBA_TOPK + k]
            phys = pt_ref[b * pages_per_req + blk * ppb + p]
            page_copy(pool_k_ref, kbuf, 0, slot, jt, phys, h).start()
            page_copy(pool_v_ref, vbuf, 1, slot, jt, phys, h).start()

    def wait_all(slot):
        for jt in range(n_cp):
            page_copy(pool_k_ref, kbuf, 0, slot, jt, 0, 0).wait()
            page_copy(pool_v_ref, vbuf, 1, slot, jt, 0, 0).wait()

    row = lax.broadcasted_iota(jnp.int32, (8, page), 0)
    row_d = lax.broadcasted_iota(jnp.int32, (8, HEAD_DIM), 0)

    def body(it, carry):
        slot = it % 2
        b, h = it // n_heads, it % n_heads

        @pl.when(it + 1 < n_it)
        def _():
            start_all(it + 1, 1 - slot)

        wait_all(slot)
        q8 = q_ref[b, h]
        bnear = bnear_ref[h]
        scores = []
        for j in range(n_sel):
            s_j = jnp.zeros((8, page), F32)
            for t in range(t_new):
                res = jnp.dot(q8, kbuf[slot, t * n_sel + j], preferred_element_type=F32)
                s_j = jnp.where(row == t, res, s_j)
            bias = jnp.broadcast_to(cfar_ref[h], (8, page))
            if j % ppb == ppb - 1:
                for t in range(t_new):
                    blk = sel_ref[(it * t_new + t) * MOBA_TOPK + j // ppb]
                    bias = jnp.where((row == t) & (blk == last_block), bnear, bias)
            scores.append(s_j + bias)
        s = jnp.concatenate(scores, axis=-1)
        knew, vnew = knew_ref[b, h], vnew_ref[b, h]
        own = [jnp.sum(q8 * knew[t2:t2 + 1, :], axis=-1, keepdims=True) + bown_ref[h, t2]
               for t2 in range(t_new)]
        m = jnp.max(s, axis=-1, keepdims=True)
        for c in own:
            m = jnp.maximum(m, c)
        p = jnp.exp(s - m)
        l = jnp.sum(p, axis=-1, keepdims=True)
        acc = jnp.zeros((8, HEAD_DIM), F32)
        for t2, c in enumerate(own):
            p_own = jnp.exp(c - m)
            l = l + p_own
            acc = acc + p_own * vnew[t2:t2 + 1, :]
        for j in range(n_sel):
            for t in range(t_new):
                res = _nt(p[:, j * page:(j + 1) * page], vbuf[slot, t * n_sel + j])
                acc = acc + jnp.where(row_d == t, res, 0.0)
        o_ref[b, h] = jnp.where(row_d < t_new, acc / l, 0.0)
        return carry

    start_all(0, 0)
    lax.fori_loop(0, n_it, body, 0)


def _moba_sample(page_table, sel, q8, knew8, vnew8, bown, bnear, cfar, pool_k_t, pool_v_t):
    db, n_pages = page_table.shape
    page = pool_k_t.shape[3]
    t_new = sel.shape[2]
    ppb = MOBA_BLOCK // page
    last_block = n_pages // ppb - 1
    n_cp = t_new * MOBA_TOPK * ppb
    whole = lambda a: pl.BlockSpec(a.shape, lambda i, pt, sl: (0,) * a.ndim)
    hbm = pl.BlockSpec(memory_space=pl.ANY)
    out_shape = jax.ShapeDtypeStruct((db, MOBA_HEADS, 8, HEAD_DIM), F32)
    grid_spec = pltpu.PrefetchScalarGridSpec(
        num_scalar_prefetch=2,
        grid=(1,),
        in_specs=[whole(q8), whole(knew8), whole(vnew8), whole(bown), whole(bnear), whole(cfar), hbm, hbm],
        out_specs=pl.BlockSpec(out_shape.shape, lambda i, pt, sl: (0, 0, 0, 0)),
        scratch_shapes=[pltpu.VMEM((2, n_cp, HEAD_DIM, page), F32),
                        pltpu.VMEM((2, n_cp, HEAD_DIM, page), F32),
                        pltpu.SemaphoreType.DMA((2, 2, n_cp))],
    )
    return pl.pallas_call(
        functools.partial(_moba_sample_kernel, ppb, last_block, t_new, n_pages),
        grid_spec=grid_spec,
        out_shape=out_shape,
        compiler_params=pltpu.CompilerParams(dimension_semantics=("arbitrary",)),
    )(page_table.reshape(-1), sel.reshape(-1), q8, knew8, vnew8, bown, bnear, cfar, pool_k_t, pool_v_t)


def kernel(x_prompt, x_sample, cache_k_a, cache_v_a, cache_k_b, cache_v_b, page_table, g_mix, w_in,
           w_merge_gate, b_merge_gate, w_branch_a, w_branch_b, w_out, lambda_q1, lambda_k1, lambda_q2,
           lambda_k2, subln_w, g_ffn, w_ffn_up, w_ffn_down, rel_bias, g_final):
    batch, seq, d = x_prompt.shape
    db, t_new, _ = x_sample.shape
    depth, n_pool, page = cache_k_a.shape[:3]
    n_pages = page_table.shape[1]
    past = n_pages * page
    nb_past = past // MOBA_BLOCK
    w = ATT_WIDTH
    assert depth == 1 and seq % TQ == 0 and past % MOBA_BLOCK == 0 and MOBA_BLOCK % page == 0
    assert n_pages % PAGES_PER_STEP == 0 and MOBA_TOPK <= nb_past <= LANES and t_new <= 8
    assert page == LANES and PAGES_PER_STEP % (MOBA_BLOCK // page) == 0

    tab = rel_bias.astype(F32).T
    tab_a, tab_b = tab[:MOBA_HEADS], tab[MOBA_HEADS:]
    far_a, far_b = tab_a[:, REL_BUCKETS - 1], tab_b[:, REL_BUCKETS - 1]
    lam_params = jnp.concatenate([lambda_q1, lambda_k1, lambda_q2, lambda_k2], axis=0).astype(F32)
    subln = subln_w.astype(F32).reshape(1, 2 * HEAD_DIM)
    w_in_b = w_in[0].astype(BF16)
    merge_weights = (g_mix.reshape(1, d), w_merge_gate[0].astype(BF16), b_merge_gate.reshape(1, 2 * d),
                     w_branch_a[0].astype(BF16), w_branch_b[0].astype(BF16), w_out[0].astype(BF16),
                     g_ffn.reshape(1, d), w_ffn_up[0].astype(BF16), w_ffn_down[0].astype(BF16),
                     g_final.reshape(1, d))

    xp = x_prompt.reshape(batch * seq, d)
    scale = HEAD_DIM ** -0.5
    qk, ka_t, va_t, kb, vb, vat, vbt, kmean = _project(xp, g_mix, w_in_b, TQ, seq // TQ, scale * LOG2E)
    bias_a, cfar_a = _prompt_bias(tab_a, [(2 * u, 2 * u + 1) for u in range(UNITS)])
    bias_b, cfar_b = _prompt_bias(tab_b, [(u, u) for u in range(UNITS)])
    o_a = _prompt_attention(True, qk, vat, 0, batch, seq, bias_a, cfar_a,
                            (kmean.reshape(batch, seq // TQ, w),))
    o_b = _prompt_attention(False, qk, vbt, 2, batch, seq, bias_b, cfar_b,
                            (lam_params, subln.reshape(2 * HEAD_DIM, 1)))
    y_prompt = _merge_ffn(xp, o_a, o_b, merge_weights, MERGE_TM).reshape(batch, seq, d)

    n_s = db * t_new
    xs = x_sample.reshape(n_s, d)
    qk_s, ka_st, va_st, kb_s, vb_s, _, _, _ = _project(xs, g_mix, w_in_b, n_s, 1, scale)
    heads_first = lambda a: a.reshape(MOBA_HEADS, HEAD_DIM, db, t_new).transpose(2, 0, 3, 1)
    ka_s, va_s = heads_first(ka_st), heads_first(va_st)
    qk_s3 = qk_s.reshape(db, t_new, 4 * w).astype(F32)
    pad_to = lambda a, axis, n: jnp.pad(a, [(0, n - a.shape[i] if i == axis else 0) for i in range(a.ndim)])
    hb = DIFF_HEADS
    pool_kb = cache_k_b.reshape(n_pool, page * hb, 2 * HEAD_DIM)
    pool_vb = cache_v_b.reshape(n_pool, page * hb, 2 * HEAD_DIM)
    pool_ka_t = jnp.transpose(cache_k_a[0], (0, 2, 3, 1))
    pool_va_t = jnp.transpose(cache_v_a[0], (0, 2, 3, 1))

    lane = np.arange(LANES)
    qb16 = qk_s3[:, :, 2 * w:3 * w].reshape(db, t_new, hb, LANES).transpose(0, 2, 1, 3)
    qb16 = qb16.reshape(db, hb * t_new, LANES)
    qb_rows = jnp.concatenate([jnp.where(lane < HEAD_DIM, qb16, 0.0),
                               jnp.where(lane >= HEAD_DIM, qb16, 0.0)], axis=1)
    rows_b = 2 * hb * t_new
    head_of_row = (np.arange(rows_b) // t_new) % hb
    tok_of_row = np.arange(rows_b) % t_new
    head_mask = lambda cols: jnp.asarray(head_of_row[:, None] == (cols % hb)[None, :])
    tab_rows = tab_b[head_of_row]
    cols_new = np.arange(LANES)
    d_new = np.where(cols_new[None, :] < t_new * hb, tok_of_row[:, None] - (cols_new // hb)[None, :], -1)
    bnew = jnp.where(head_mask(cols_new), _bias_from_dist(tab_rows, d_new, per_row=True), NEG_INF)
    span = PAGES_PER_STEP * page
    cols = np.arange(span * hb)
    d_last = (past + tok_of_row)[:, None] - (past - span + cols // hb)[None, :]
    bias_s = jnp.stack([jnp.broadcast_to(far_b[head_of_row][:, None], (rows_b, span * hb)),
                        _bias_from_dist(tab_rows, d_last, per_row=True)])
    bias_s = jnp.where(head_mask(cols)[None], bias_s, NEG_INF)
    seg = np.arange(w) // HEAD_DIM
    qa_s = qk_s3[:, :, 0:w]
    qa_rows = jnp.where(jnp.asarray(seg[None, None, :] == np.arange(MOBA_HEADS)[:, None, None]),
                        qa_s[:, None], 0.0).reshape(db, MOBA_HEADS * t_new, w)
    knew_b = pad_to(kb_s.reshape(db, t_new * hb, 2 * HEAD_DIM), 1, LANES)
    vnew_b = pad_to(vb_s.reshape(db, t_new * hb, 2 * HEAD_DIM), 1, LANES)
    o_b16, sel_rows = _sample_stream(
        page_table, qb_rows, qa_rows, knew_b, vnew_b, bnew, bias_s, lam_params, subln,
        pool_kb, pool_vb, pool_ka_t.reshape(n_pool, w, page), page)
    o_b_s = o_b16.reshape(db, hb, t_new, LANES).transpose(0, 2, 1, 3).reshape(n_s, w).astype(BF16)
    sel = sel_rows[:, :, :MOBA_TOPK].reshape(db, MOBA_HEADS, t_new, MOBA_TOPK)

    q_heads = qa_s.reshape(db, t_new, MOBA_HEADS, HEAD_DIM).transpose(0, 2, 1, 3)
    d_own = np.arange(8)[None, :] - np.arange(t_new)[:, None]
    d_own = np.where(np.arange(8)[None, :] < t_new, d_own, -1)
    bown = _bias_from_dist(tab_a, d_own)[..., None]
    d_near = (past + np.arange(8))[:, None] - (past - page + np.arange(page))[None, :]
    bnear = _bias_from_dist(tab_a, d_near)
    cfar_s = jnp.broadcast_to(far_a[:, None, None], (MOBA_HEADS, 1, page))
    o_heads = _moba_sample(page_table, sel, pad_to(q_heads, 2, 8), pad_to(ka_s, 2, 8), pad_to(va_s, 2, 8),
                           bown, bnear, cfar_s, pool_ka_t, pool_va_t)
    o_a_s = o_heads[:, :, :t_new].transpose(0, 2, 1, 3).reshape(n_s, w).astype(BF16)
    y_sample = _merge_ffn(xs, o_a_s, o_b_s, merge_weights, n_s).reshape(db, t_new, d)

    a_shape = (MOBA_HEADS, HEAD_DIM)
    b_shape = (DIFF_HEADS, 2 * HEAD_DIM)
    rows_first = lambda a: a.reshape((batch,) + a_shape + (seq,)).transpose(0, 3, 1, 2)[None]
    return (y_prompt, y_sample,
            rows_first(ka_t), rows_first(va_t),
            kb.reshape((1, batch, seq) + b_shape), vb.reshape((1, batch, seq) + b_shape),
            ka_s.transpose(0, 2, 1, 3)[None], va_s.transpose(0, 2, 1, 3)[None],
            kb_s.reshape((1, db, t_new) + b_shape), vb_s.reshape((1, db, t_new) + b_shape))
```

```python
import functools
import math

import numpy as np
import jax
import jax.numpy as jnp
from jax import lax
from jax.experimental import pallas as pl
from jax.experimental.pallas import tpu as pltpu

HEAD_DIM = 64
MOBA_HEADS = 8
DIFF_HEADS = 4
MOBA_BLOCK = 256
MOBA_TOPK = 3
REL_BUCKETS = 32
REL_MAX_DIST = 128
RMS_EPS = 1e-6
LAMBDA_INIT = 0.8 - 0.6 * math.exp(-0.3 * 0)

LANES = 128
ATT_WIDTH = 512
UNITS = ATT_WIDTH // LANES
TQ = MOBA_BLOCK
MERGE_TM = 512
PAGES_PER_STEP = 16
VMEM_LIMIT = 56 * 1024 * 1024
BF16_ROWS = 16
V_ROWS = LANES + BF16_ROWS
LOG2E = math.log2(math.e)

F32 = jnp.float32
BF16 = jnp.bfloat16
NEG_INF = float("-inf")


def _rel_bucket_np(dist):
    n = np.maximum(dist, 0)
    exact = REL_BUCKETS // 2
    nf = np.maximum(n, 1).astype(np.float64)
    large = exact + (np.log(nf / exact) / math.log(REL_MAX_DIST / exact)
                     * (REL_BUCKETS - exact)).astype(np.int64)
    return np.where(n < exact, n, np.minimum(large, REL_BUCKETS - 1)).astype(np.int32)


def _bias_from_dist(tab, dist, per_row=False):
    idx = jnp.asarray(_rel_bucket_np(dist))
    col = (lambda k: tab[:, k:k + 1]) if per_row else (
        lambda k: tab[:, k].reshape((-1,) + (1,) * dist.ndim))
    b = jnp.broadcast_to(col(0), dist.shape if per_row else (tab.shape[0],) + dist.shape)
    for k in range(1, REL_BUCKETS):
        b = jnp.where(idx == k, col(k), b)
    return jnp.where(jnp.asarray(dist >= 0), b, NEG_INF)


def _nt(a, b, **kw):
    return lax.dot_general(a, b, (((1,), (1,)), ((), ())), preferred_element_type=F32, **kw)


def _const_spec(shape):
    n = len(shape)
    return pl.BlockSpec(shape, lambda *_: (0,) * n, pipeline_mode=pl.Buffered(1))


def _with_ones_rows(vt):
    row = lax.broadcasted_iota(jnp.int32, (BF16_ROWS, vt.shape[1]), 0)
    ones = jnp.where(row == 0, 1.0, 0.0).astype(BF16)
    parts = []
    for u in range(UNITS):
        parts += [vt[u * LANES:(u + 1) * LANES].astype(BF16), ones]
    return jnp.concatenate(parts, axis=0)


def _proj_kernel(scale, x_ref, g_ref, w_ref, qk_ref, kat_ref, vat_ref, kb_ref, vb_ref, vat16_ref,
                 vbt16_ref, kmean_ref):
    x = x_ref[...]
    xn = x * lax.rsqrt(jnp.mean(x * x, axis=-1, keepdims=True) + RMS_EPS) * g_ref[...]
    p = jnp.dot(xn.astype(BF16), w_ref[...], preferred_element_type=F32)
    w = ATT_WIDTH
    ka, va = p[:, w:2 * w], p[:, 2 * w:3 * w]
    kb, vb = p[:, 4 * w:5 * w], p[:, 5 * w:6 * w]
    vat = va.T
    kat_ref[0] = ka.T
    vat_ref[0] = vat
    for h in range(DIFF_HEADS):
        kb_ref[:, h, :] = kb[:, h * LANES:(h + 1) * LANES]
        vb_ref[:, h, :] = vb[:, h * LANES:(h + 1) * LANES]
    qk_ref[:, 0:w] = (p[:, 0:w] * scale).astype(BF16)
    qk_ref[:, w:2 * w] = ka.astype(BF16)
    qk_ref[:, 2 * w:3 * w] = (p[:, 3 * w:4 * w] * scale).astype(BF16)
    qk_ref[:, 3 * w:4 * w] = kb.astype(BF16)
    vat16_ref[0] = _with_ones_rows(vat)
    vbt16_ref[0] = _with_ones_rows(vb.T)
    kmean_ref[0] = jnp.mean(ka, axis=0, keepdims=True)


def _project(x2d, g, w_bf16, tm, tiles_per_seq, q_scale):
    n, d = x2d.shape
    w = ATT_WIDTH
    steps = n // tm
    seq = tm * tiles_per_seq
    row = lambda i: (i, 0)
    f32_rows = jax.ShapeDtypeStruct((n, DIFF_HEADS, 2 * HEAD_DIM), F32)
    rows_spec = pl.BlockSpec((tm, DIFF_HEADS, 2 * HEAD_DIM), lambda i: (i, 0, 0))
    f32_t = jax.ShapeDtypeStruct((n // seq, w, seq), F32)
    t_spec = pl.BlockSpec((1, w, tm), lambda i: (i // tiles_per_seq, 0, i % tiles_per_seq))
    vt16 = jax.ShapeDtypeStruct((steps, UNITS * V_ROWS, tm), BF16)
    return pl.pallas_call(
        functools.partial(_proj_kernel, q_scale),
        grid=(steps,),
        in_specs=[pl.BlockSpec((tm, d), row), _const_spec((1, d)), _const_spec((d, 6 * w))],
        out_specs=[pl.BlockSpec((tm, 4 * w), row), t_spec, t_spec,
                   rows_spec, rows_spec]
                  + [pl.BlockSpec((1, UNITS * V_ROWS, tm), lambda i: (i, 0, 0))] * 2
                  + [pl.BlockSpec((1, 1, w), lambda i: (i, 0, 0))],
        out_shape=[jax.ShapeDtypeStruct((n, 4 * w), BF16), f32_t, f32_t, f32_rows, f32_rows,
                   vt16, vt16, jax.ShapeDtypeStruct((steps, 1, w), F32)],
        compiler_params=pltpu.CompilerParams(dimension_semantics=("arbitrary",),
                                             vmem_limit_bytes=VMEM_LIMIT),
    )(x2d, g.reshape(1, d), w_bf16)


def _split_halves(q2):
    lane = lax.broadcasted_iota(jnp.int32, q2.shape, 1)
    zero = jnp.zeros_like(q2)
    return jnp.concatenate([jnp.where(lane < HEAD_DIM, q2, zero),
                            jnp.where(lane >= HEAD_DIM, q2, zero)], axis=0)


def _online_update(u, s, shift, vt, m_ref, acc_ref):
    m_old = m_ref[u]
    m_new = jnp.maximum(m_old, jnp.max(s, axis=0, keepdims=True) + shift)
    p = jnp.exp2(s - (m_new - shift))
    alpha = jnp.exp2(m_old - m_new)
    acc_ref[u] = alpha * acc_ref[u] + jnp.dot(vt, p.astype(BF16), preferred_element_type=F32)
    m_ref[u] = m_new


def _units_ahead(score_fn, fold_fn, first=None, after=None):
    s_next = score_fn(0) if first is None else first
    for u in range(UNITS):
        s = s_next
        if u + 1 < UNITS:
            s_next = score_fn(u + 1)
        elif after is not None:
            s_next = after()
        fold_fn(u, s)
    return s_next


def _attn_kernel(moba, *refs):
    if moba:
        (q_ref, k_ref, vt_ref, bias_ref, cfar_ref, kmean_ref, o_ref,
         qs_ref, m_ref, acc_ref, sh_ref) = refs
    else:
        (q_ref, k_ref, vt_ref, bias_ref, cfar_ref, lam_ref, subln_ref, o_ref,
         qs_ref, m_ref, acc_ref) = refs
    i = pl.program_id(1)
    tq = TQ

    def k_tile(n, u):
        return k_ref[pl.ds(pl.multiple_of(n * tq, tq), tq), u * LANES:(u + 1) * LANES]

    def vt_tile(n, u):
        return vt_ref[n, u * V_ROWS:(u + 1) * V_ROWS, :]

    def scores(n):
        return lambda u: _nt(k_tile(n, u), qs_ref[u])

    for u in range(UNITS):
        qs = _split_halves(q_ref[:, u * LANES:(u + 1) * LANES])
        qs_ref[u] = qs
        if moba:
            km = kmean_ref[0, :, u * LANES:(u + 1) * LANES]
            hi = km.astype(BF16)
            r1 = km - hi.astype(F32)
            mid = r1.astype(BF16)
            lo = (r1 - mid.astype(F32)).astype(BF16)
            g = _nt(hi, qs) + _nt(mid, qs) + _nt(lo, qs)
            nb = g.shape[0]
            blk = lax.broadcasted_iota(jnp.int32, g.shape, 0)
            valid = blk < i
            g = jnp.where(valid, g, NEG_INF)
            rank = jnp.zeros(g.shape, jnp.int32)
            for mm in range(nb):
                gm = g[mm:mm + 1, :]
                ahead = (gm > g) | ((gm == g) & (mm < blk))
                rank = rank + ahead.astype(jnp.int32)
            sel = valid & (rank < MOBA_TOPK)
            sh_ref[u] = jnp.where(sel, cfar_ref[u], NEG_INF)

    def fold_own(u, s):
        s = s + bias_ref[u, 0]
        m = jnp.max(s, axis=0, keepdims=True)
        p = jnp.exp2(s - m)
        m_ref[u] = m
        acc_ref[u] = jnp.dot(vt_tile(i, u), p.astype(BF16), preferred_element_type=F32)

    _units_ahead(scores(i), fold_own)

    @pl.when(i >= 1)
    def _():
        def fold_prev(u, s):
            if moba:
                shift = sh_ref[u, pl.ds(i - 1, 1), :] - cfar_ref[u]
            else:
                shift = jnp.zeros((1, 2 * tq), F32)
            _online_update(u, s + bias_ref[u, 1], shift, vt_tile(i - 1, u), m_ref, acc_ref)

        _units_ahead(scores(i - 1), fold_prev)

    def far(n, s_first):
        def fold_far(u, s):
            shift = sh_ref[u, pl.ds(n, 1), :] if moba else cfar_ref[u]
            _online_update(u, s, shift, vt_tile(n, u), m_ref, acc_ref)

        nxt = jnp.minimum(n + 1, i - 2)
        return _units_ahead(scores(n), fold_far, first=s_first, after=lambda: scores(nxt)(0))

    lax.fori_loop(0, i - 1, far, scores(0)(0))

    chan = lax.broadcasted_iota(jnp.int32, (LANES, tq), 0)
    if not moba:
        lp = lam_ref[...]
        lam = (jnp.exp(jnp.sum(lp[0:1] * lp[1:2], axis=-1, keepdims=True))
               - jnp.exp(jnp.sum(lp[2:3] * lp[3:4], axis=-1, keepdims=True)) + LAMBDA_INIT)
    for u in range(UNITS):
        acc = acc_ref[u]
        o = acc[:LANES] / acc[LANES:LANES + 1]
        if moba:
            out = jnp.where(chan < HEAD_DIM, o[:, :tq], o[:, tq:])
        else:
            out = o[:, :tq] - lam * o[:, tq:]
            out = out * lax.rsqrt(jnp.mean(out * out, axis=0, keepdims=True) + RMS_EPS)
            out = out * subln_ref[...] * (1.0 - LAMBDA_INIT)
        o_ref[:, u * LANES:(u + 1) * LANES] = out.T.astype(o_ref.dtype)


def _prompt_attention(moba, qk, vt, col0, batch, seq, bias, cfar, extra):
    nt = seq // TQ
    w = ATT_WIDTH
    in_specs = [
        pl.BlockSpec((TQ, w), lambda b, i: (b * nt + i, col0)),
        pl.BlockSpec((seq, w), lambda b, i: (b, col0 + 1)),
        pl.BlockSpec((nt, UNITS * V_ROWS, TQ), lambda b, i: (b, 0, 0)),
        _const_spec(bias.shape),
        _const_spec(cfar.shape),
    ]
    scratch = [pltpu.VMEM((UNITS, 2 * TQ, LANES), BF16),
               pltpu.VMEM((UNITS, 1, 2 * TQ), F32),
               pltpu.VMEM((UNITS, V_ROWS, 2 * TQ), F32)]
    if moba:
        (kmean,) = extra
        in_specs.append(pl.BlockSpec((1, nt, w), lambda b, i: (b, 0, 0)))
        scratch.append(pltpu.VMEM((UNITS, nt, 2 * TQ), F32))
    else:
        lam_params, subln_col = extra
        in_specs += [_const_spec(lam_params.shape), _const_spec(subln_col.shape)]
    return pl.pallas_call(
        functools.partial(_attn_kernel, moba),
        grid=(batch, nt),
        in_specs=in_specs,
        out_specs=pl.BlockSpec((TQ, w), lambda b, i: (b * nt + i, 0)),
        out_shape=jax.ShapeDtypeStruct((batch * seq, w), BF16),
        scratch_shapes=scratch,
        compiler_params=pltpu.CompilerParams(dimension_semantics=("arbitrary", "arbitrary"),
                                             vmem_limit_bytes=VMEM_LIMIT),
    )(qk, qk, vt, bias, cfar, *extra)


def _prompt_bias(tab, heads_of_unit):
    r = np.arange(TQ)[None, :] - np.arange(TQ)[:, None]
    own = _bias_from_dist(tab, r)
    prev = _bias_from_dist(tab, r + TQ)
    far = tab[:, REL_BUCKETS - 1]
    bias, cfar = [], []
    for h0, h1 in heads_of_unit:
        bias.append(jnp.stack([jnp.concatenate([own[h0], own[h1]], axis=1),
                               jnp.concatenate([prev[h0], prev[h1]], axis=1)]))
        cfar.append(jnp.concatenate([jnp.full((1, TQ), far[h0]), jnp.full((1, TQ), far[h1])], axis=1))
    return jnp.stack(bias) * LOG2E, jnp.stack(cfar) * LOG2E


def _merge_ffn_kernel(ffn_chunk, x_ref, oa_ref, ob_ref, gmix_ref, wg_ref, bg_ref, wa_ref, wb_ref,
                      wo_ref, gffn_ref, wup_ref, wdn_ref, gfin_ref, y_ref):
    x = x_ref[...]
    d = x.shape[1]
    xn = (x * lax.rsqrt(jnp.mean(x * x, axis=-1, keepdims=True) + RMS_EPS) * gmix_ref[...]).astype(BF16)
    gates = jax.nn.sigmoid(jnp.dot(xn, wg_ref[...], preferred_element_type=F32) + bg_ref[...])
    merged = (gates[:, :d] * jnp.dot(oa_ref[...], wa_ref[...], preferred_element_type=F32)
              + gates[:, d:] * jnp.dot(ob_ref[...], wb_ref[...], preferred_element_type=F32))
    h = x + jnp.dot(merged.astype(BF16), wo_ref[...], preferred_element_type=F32)
    hn = (h * lax.rsqrt(jnp.mean(h * h, axis=-1, keepdims=True) + RMS_EPS) * gffn_ref[...]).astype(BF16)
    hidden = wdn_ref.shape[0]
    out = h
    for c in range(hidden // ffn_chunk):
        lo = c * ffn_chunk
        gate = jnp.dot(hn, wup_ref[:, lo:lo + ffn_chunk], preferred_element_type=F32)
        up = jnp.dot(hn, wup_ref[:, hidden + lo:hidden + lo + ffn_chunk], preferred_element_type=F32)
        act = (gate * jax.nn.sigmoid(gate) * up).astype(BF16)
        out = out + jnp.dot(act, wdn_ref[lo:lo + ffn_chunk, :], preferred_element_type=F32)
    y = out * lax.rsqrt(jnp.mean(out * out, axis=-1, keepdims=True) + RMS_EPS) * gfin_ref[...]
    y_ref[...] = y


def _merge_ffn(x2d, oa, ob, weights, tm):
    n, d = x2d.shape
    row = lambda i: (i, 0)
    in_specs = [pl.BlockSpec((tm, d), row), pl.BlockSpec((tm, ATT_WIDTH), row),
                pl.BlockSpec((tm, ATT_WIDTH), row)] + [_const_spec(w.shape) for w in weights]
    hidden = weights[-2].shape[0]
    return pl.pallas_call(
        functools.partial(_merge_ffn_kernel, math.gcd(hidden, 2 * LANES)),
        grid=(n // tm,),
        in_specs=in_specs,
        out_specs=pl.BlockSpec((tm, d), row),
        out_shape=jax.ShapeDtypeStruct((n, d), F32),
        compiler_params=pltpu.CompilerParams(dimension_semantics=("arbitrary",),
                                             vmem_limit_bytes=VMEM_LIMIT),
    )(x2d, oa, ob, *weights)


def _top_lanes(g, count):
    lane = lax.broadcasted_iota(jnp.int32, g.shape, 1)
    out = jnp.zeros(g.shape, jnp.int32)
    for k in range(count):
        best = jnp.max(g, axis=-1, keepdims=True)
        idx = jnp.min(jnp.where(g == best, lane, LANES), axis=-1, keepdims=True)
        out = jnp.where(lane == k, idx, out)
        g = jnp.where(lane == idx, NEG_INF, g)
    return out


def _stream_kernel(ppb, pt_ref, qb_ref, qa_ref, knew_ref, vnew_ref, bnew_ref, bias_ref, lam_ref,
                   subln_ref, *refs):
    del pt_ref
    pps = PAGES_PER_STEP
    kb_refs, vb_refs, ka_refs = refs[:pps], refs[pps:2 * pps], refs[2 * pps:3 * pps]
    ob_ref, idx_ref, m_ref, l_ref, acc_ref, gate_ref = refs[3 * pps:]
    c = pl.program_id(1)
    nc = pl.num_programs(1)
    qb = qb_ref[0]
    half = qb.shape[0] // 2

    @pl.when(c == 0)
    def _():
        s = _nt(qb, knew_ref[0]) + bnew_ref[...]
        m = jnp.max(s, axis=-1, keepdims=True)
        p = jnp.exp(s - m)
        m_ref[...] = m
        l_ref[...] = jnp.sum(p, axis=-1, keepdims=True)
        acc_ref[...] = jnp.dot(p, vnew_ref[0], preferred_element_type=F32)
        gate_ref[...] = jnp.full(gate_ref.shape, NEG_INF, F32)

    s = jnp.concatenate([_nt(qb, kr[...]) for kr in kb_refs], axis=-1) + bias_ref[0]
    m_old = m_ref[...]
    m_new = jnp.maximum(m_old, jnp.max(s, axis=-1, keepdims=True))
    p = jnp.exp(s - m_new)
    alpha = jnp.exp(m_old - m_new)
    l_ref[...] = alpha * l_ref[...] + jnp.sum(p, axis=-1, keepdims=True)
    width = kb_refs[0].shape[0]
    pv = jnp.dot(p[:, :width], vb_refs[0][...], preferred_element_type=F32)
    for j in range(1, pps):
        pv = pv + jnp.dot(p[:, j * width:(j + 1) * width], vb_refs[j][...], preferred_element_type=F32)
    acc_ref[...] = alpha * acc_ref[...] + pv
    m_ref[...] = m_new

    qa = qa_ref[0]
    gate = gate_ref[...]
    lane = lax.broadcasted_iota(jnp.int32, gate.shape, 1)
    for j in range(pps // ppb):
        sa = jnp.dot(qa, ka_refs[j * ppb][...], preferred_element_type=F32)
        for jj in range(1, ppb):
            sa = sa + jnp.dot(qa, ka_refs[j * ppb + jj][...], preferred_element_type=F32)
        g = jnp.sum(sa, axis=-1, keepdims=True)
        gate = jnp.where(lane == c * (pps // ppb) + j, g, gate)
    gate_ref[...] = gate

    @pl.when(c == nc - 1)
    def _():
        lp = lam_ref[...]
        lam = (jnp.exp(jnp.sum(lp[0:1] * lp[1:2], axis=-1, keepdims=True))
               - jnp.exp(jnp.sum(lp[2:3] * lp[3:4], axis=-1, keepdims=True)) + LAMBDA_INIT)
        o = acc_ref[...] / l_ref[...]
        out = o[:half] - lam * o[half:]
        out = out * lax.rsqrt(jnp.mean(out * out, axis=-1, keepdims=True) + RMS_EPS)
        ob_ref[0] = out * subln_ref[...] * (1.0 - LAMBDA_INIT)
        idx_ref[0] = _top_lanes(gate, MOBA_TOPK)


def _sample_stream(page_table, qb_rows, qa_rows, knew, vnew, bnew, bias, lam_params, subln,
                   pool_kb, pool_vb, pool_ka_t, page):
    db, n_pages = page_table.shape
    pps = PAGES_PER_STEP
    nc = n_pages // pps
    rows_b, rows_a = qb_rows.shape[1], qa_rows.shape[1]
    blk_b, blk_a = pool_kb.shape[1:], pool_ka_t.shape[1:]

    def page_spec(j, blk):
        return pl.BlockSpec((None,) + blk, lambda b, c, pt: (pt[b * n_pages + c * pps + j], 0, 0))

    per_b = lambda b, c, pt: (b, 0, 0)
    fixed = lambda b, c, pt: (0, 0)
    in_specs = [
        pl.BlockSpec((1,) + qb_rows.shape[1:], per_b),
        pl.BlockSpec((1,) + qa_rows.shape[1:], per_b),
        pl.BlockSpec((1,) + knew.shape[1:], per_b),
        pl.BlockSpec((1,) + vnew.shape[1:], per_b),
        pl.BlockSpec(bnew.shape, fixed),
        pl.BlockSpec((1,) + bias.shape[1:], lambda b, c, pt: (jnp.where(c == nc - 1, 1, 0), 0, 0)),
        pl.BlockSpec(lam_params.shape, fixed),
        pl.BlockSpec(subln.shape, fixed),
    ] + [page_spec(j, blk_b) for j in range(pps)] * 2 + [page_spec(j, blk_a) for j in range(pps)]
    grid_spec = pltpu.PrefetchScalarGridSpec(
        num_scalar_prefetch=1,
        grid=(db, nc),
        in_specs=in_specs,
        out_specs=[pl.BlockSpec((1, rows_b // 2, LANES), per_b), pl.BlockSpec((1, rows_a, LANES), per_b)],
        scratch_shapes=[pltpu.VMEM((rows_b, 1), F32), pltpu.VMEM((rows_b, 1), F32),
                        pltpu.VMEM((rows_b, LANES), F32), pltpu.VMEM((rows_a, LANES), F32)],
    )
    return pl.pallas_call(
        functools.partial(_stream_kernel, MOBA_BLOCK // page),
        grid_spec=grid_spec,
        out_shape=[jax.ShapeDtypeStruct((db, rows_b // 2, LANES), F32),
                   jax.ShapeDtypeStruct((db, rows_a, LANES), jnp.int32)],
        compiler_params=pltpu.CompilerParams(dimension_semantics=("arbitrary", "arbitrary"),
                                             vmem_limit_bytes=VMEM_LIMIT),
    )(page_table.reshape(-1), qb_rows, qa_rows, knew, vnew, bnew, bias, lam_params, subln,
      *([pool_kb] * pps), *([pool_vb] * pps), *([pool_ka_t] * pps))


def _moba_sample_kernel(ppb, last_block, t_new, pages_per_req, pt_ref, sel_ref, q_ref, knew_ref,
                        vnew_ref, bown_ref, bnear_ref, cfar_ref, pool_k_ref, pool_v_ref, o_ref,
                        kbuf, vbuf, sem):
    n_sel = ppb * MOBA_TOPK
    n_cp = t_new * n_sel
    n_heads = q_ref.shape[1]
    n_it = q_ref.shape[0] * n_heads
    page = kbuf.shape[3]

    def page_copy(pool_ref, buf, kind, slot, jt, phys, h):
        return pltpu.make_async_copy(pool_ref.at[phys, h], buf.at[slot, jt], sem.at[slot, kind, jt])

    def start_all(it, slot):
        b, h = it // n_heads, it % n_heads
        for jt in range(n_cp):
            t, k, p = jt // n_sel, (jt % n_sel) // ppb, jt % ppb
            blk = sel_ref[(it * t_new + t) * MOBA_TOPK + k]
            phys = pt_ref[b * pages_per_req + blk * ppb + p]
            page_copy(pool_k_ref, kbuf, 0, slot, jt, phys, h).start()
            page_copy(pool_v_ref, vbuf, 1, slot, jt, phys, h).start()

    def wait_all(slot):
        for jt in range(n_cp):
            page_copy(pool_k_ref, kbuf, 0, slot, jt, 0, 0).wait()
            page_copy(pool_v_ref, vbuf, 1, slot, jt, 0, 0).wait()

    row = lax.broadcasted_iota(jnp.int32, (8, page), 0)
    row_d = lax.broadcasted_iota(jnp.int32, (8, HEAD_DIM), 0)

    def body(it, carry):
        slot = it % 2
        b, h = it // n_heads, it % n_heads

        @pl.when(it + 1 < n_it)
        def _():
            start_all(it + 1, 1 - slot)

        wait_all(slot)
        q8 = q_ref[b, h]
        bnear = bnear_ref[h]
        scores = []
        for j in range(n_sel):
            s_j = jnp.zeros((8, page), F32)
            for t in range(t_new):
                res = jnp.dot(q8, kbuf[slot, t * n_sel + j], preferred_element_type=F32)
                s_j = jnp.where(row == t, res, s_j)
            bias = jnp.broadcast_to(cfar_ref[h], (8, page))
            if j % ppb == ppb - 1:
                for t in range(t_new):
                    blk = sel_ref[(it * t_new + t) * MOBA_TOPK + j // ppb]
                    bias = jnp.where((row == t) & (blk == last_block), bnear, bias)
            scores.append(s_j + bias)
        s = jnp.concatenate(scores, axis=-1)
        knew, vnew = knew_ref[b, h], vnew_ref[b, h]
        own = [jnp.sum(q8 * knew[t2:t2 + 1, :], axis=-1, keepdims=True) + bown_ref[h, t2]
               for t2 in range(t_new)]
        m = jnp.max(s, axis=-1, keepdims=True)
        for c in own:
            m = jnp.maximum(m, c)
        p = jnp.exp(s - m)
        l = jnp.sum(p, axis=-1, keepdims=True)
        acc = jnp.zeros((8, HEAD_DIM), F32)
        for t2, c in enumerate(own):
            p_own = jnp.exp(c - m)
            l = l + p_own
            acc = acc + p_own * vnew[t2:t2 + 1, :]
        for j in range(n_sel):
            for t in range(t_new):
                res = _nt(p[:, j * page:(j + 1) * page], vbuf[slot, t * n_sel + j])
                acc = acc + jnp.where(row_d == t, res, 0.0)
        o_ref[b, h] = jnp.where(row_d < t_new, acc / l, 0.0)
        return carry

    start_all(0, 0)
    lax.fori_loop(0, n_it, body, 0)


def _moba_sample(page_table, sel, q8, knew8, vnew8, bown, bnear, cfar, pool_k_t, pool_v_t):
    db, n_pages = page_table.shape
    page = pool_k_t.shape[3]
    t_new = sel.shape[2]
    ppb = MOBA_BLOCK // page
    last_block = n_pages // ppb - 1
    n_cp = t_new * MOBA_TOPK * ppb
    whole = lambda a: pl.BlockSpec(a.shape, lambda i, pt, sl: (0,) * a.ndim)
    hbm = pl.BlockSpec(memory_space=pl.ANY)
    out_shape = jax.ShapeDtypeStruct((db, MOBA_HEADS, 8, HEAD_DIM), F32)
    grid_spec = pltpu.PrefetchScalarGridSpec(
        num_scalar_prefetch=2,
        grid=(1,),
        in_specs=[whole(q8), whole(knew8), whole(vnew8), whole(bown), whole(bnear), whole(cfar), hbm, hbm],
        out_specs=pl.BlockSpec(out_shape.shape, lambda i, pt, sl: (0, 0, 0, 0)),
        scratch_shapes=[pltpu.VMEM((2, n_cp, HEAD_DIM, page), F32),
                        pltpu.VMEM((2, n_cp, HEAD_DIM, page), F32),
                        pltpu.SemaphoreType.DMA((2, 2, n_cp))],
    )
    return pl.pallas_call(
        functools.partial(_moba_sample_kernel, ppb, last_block, t_new, n_pages),
        grid_spec=grid_spec,
        out_shape=out_shape,
        compiler_params=pltpu.CompilerParams(dimension_semantics=("arbitrary",)),
    )(page_table.reshape(-1), sel.reshape(-1), q8, knew8, vnew8, bown, bnear, cfar, pool_k_t, pool_v_t)


def kernel(x_prompt, x_sample, cache_k_a, cache_v_a, cache_k_b, cache_v_b, page_table, g_mix, w_in,
           w_merge_gate, b_merge_gate, w_branch_a, w_branch_b, w_out, lambda_q1, lambda_k1, lambda_q2,
           lambda_k2, subln_w, g_ffn, w_ffn_up, w_ffn_down, rel_bias, g_final):
    batch, seq, d = x_prompt.shape
    db, t_new, _ = x_sample.shape
    depth, n_pool, page = cache_k_a.shape[:3]
    n_pages = page_table.shape[1]
    past = n_pages * page
    nb_past = past // MOBA_BLOCK
    w = ATT_WIDTH
    assert depth == 1 and seq % TQ == 0 and past % MOBA_BLOCK == 0 and MOBA_BLOCK % page == 0
    assert n_pages % PAGES_PER_STEP == 0 and MOBA_TOPK <= nb_past <= LANES and t_new <= 8
    assert page == LANES and PAGES_PER_STEP % (MOBA_BLOCK // page) == 0

    tab = rel_bias.astype(F32).T
    tab_a, tab_b = tab[:MOBA_HEADS], tab[MOBA_HEADS:]
    far_a, far_b = tab_a[:, REL_BUCKETS - 1], tab_b[:, REL_BUCKETS - 1]
    lam_params = jnp.concatenate([lambda_q1, lambda_k1, lambda_q2, lambda_k2], axis=0).astype(F32)
    subln = subln_w.astype(F32).reshape(1, 2 * HEAD_DIM)
    w_in_b = w_in[0].astype(BF16)
    merge_weights = (g_mix.reshape(1, d), w_merge_gate[0].astype(BF16), b_merge_gate.reshape(1, 2 * d),
                     w_branch_a[0].astype(BF16), w_branch_b[0].astype(BF16), w_out[0].astype(BF16),
                     g_ffn.reshape(1, d), w_ffn_up[0].astype(BF16), w_ffn_down[0].astype(BF16),
                     g_final.reshape(1, d))

    xp = x_prompt.reshape(batch * seq, d)
    scale = HEAD_DIM ** -0.5
    qk, ka_t, va_t, kb, vb, vat, vbt, kmean = _project(xp, g_mix, w_in_b, TQ, seq // TQ, scale * LOG2E)
    bias_a, cfar_a = _prompt_bias(tab_a, [(2 * u, 2 * u + 1) for u in range(UNITS)])
    bias_b, cfar_b = _prompt_bias(tab_b, [(u, u) for u in range(UNITS)])
    o_a = _prompt_attention(True, qk, vat, 0, batch, seq, bias_a, cfar_a,
                            (kmean.reshape(batch, seq // TQ, w),))
    o_b = _prompt_attention(False, qk, vbt, 2, batch, seq, bias_b, cfar_b,
                            (lam_params, subln.reshape(2 * HEAD_DIM, 1)))
    y_prompt = _merge_ffn(xp, o_a, o_b, merge_weights, MERGE_TM).reshape(batch, seq, d)

    n_s = db * t_new
    xs = x_sample.reshape(n_s, d)
    qk_s, ka_st, va_st, kb_s, vb_s, _, _, _ = _project(xs, g_mix, w_in_b, n_s, 1, scale)
    heads_first = lambda a: a.reshape(MOBA_HEADS, HEAD_DIM, db, t_new).transpose(2, 0, 3, 1)
    ka_s, va_s = heads_first(ka_st), heads_first(va_st)
    qk_s3 = qk_s.reshape(db, t_new, 4 * w).astype(F32)
    pad_to = lambda a, axis, n: jnp.pad(a, [(0, n - a.shape[i] if i == axis else 0) for i in range(a.ndim)])
    hb = DIFF_HEADS
    pool_kb = cache_k_b.reshape(n_pool, page * hb, 2 * HEAD_DIM)
    pool_vb = cache_v_b.reshape(n_pool, page * hb, 2 * HEAD_DIM)
    pool_ka_t = jnp.transpose(cache_k_a[0], (0, 2, 3, 1))
    pool_va_t = jnp.transpose(cache_v_a[0], (0, 2, 3, 1))

    lane = np.arange(LANES)
    qb16 = qk_s3[:, :, 2 * w:3 * w].reshape(db, t_new, hb, LANES).transpose(0, 2, 1, 3)
    qb16 = qb16.reshape(db, hb * t_new, LANES)
    qb_rows = jnp.concatenate([jnp.where(lane < HEAD_DIM, qb16, 0.0),
                               jnp.where(lane >= HEAD_DIM, qb16, 0.0)], axis=1)
    rows_b = 2 * hb * t_new
    head_of_row = (np.arange(rows_b) // t_new) % hb
    tok_of_row = np.arange(rows_b) % t_new
    head_mask = lambda cols: jnp.asarray(head_of_row[:, None] == (cols % hb)[None, :])
    tab_rows = tab_b[head_of_row]
    cols_new = np.arange(LANES)
    d_new = np.where(cols_new[None, :] < t_new * hb, tok_of_row[:, None] - (cols_new // hb)[None, :], -1)
    bnew = jnp.where(head_mask(cols_new), _bias_from_dist(tab_rows, d_new, per_row=True), NEG_INF)
    span = PAGES_PER_STEP * page
    cols = np.arange(span * hb)
    d_last = (past + tok_of_row)[:, None] - (past - span + cols // hb)[None, :]
    bias_s = jnp.stack([jnp.broadcast_to(far_b[head_of_row][:, None], (rows_b, span * hb)),
                        _bias_from_dist(tab_rows, d_last, per_row=True)])
    bias_s = jnp.where(head_mask(cols)[None], bias_s, NEG_INF)
    seg = np.arange(w) // HEAD_DIM
    qa_s = qk_s3[:, :, 0:w]
    qa_rows = jnp.where(jnp.asarray(seg[None, None, :] == np.arange(MOBA_HEADS)[:, None, None]),
                        qa_s[:, None], 0.0).reshape(db, MOBA_HEADS * t_new, w)
    knew_b = pad_to(kb_s.reshape(db, t_new * hb, 2 * HEAD_DIM), 1, LANES)
    vnew_b = pad_to(vb_s.reshape(db, t_new * hb, 2 * HEAD_DIM), 1, LANES)
    o_b16, sel_rows = _sample_stream(
        page_table, qb_rows, qa_rows, knew_b, vnew_b, bnew, bias_s, lam_params, subln,
        pool_kb, pool_vb, pool_ka_t.reshape(n_pool, w, page), page)
    o_b_s = o_b16.reshape(db, hb, t_new, LANES).transpose(0, 2, 1, 3).reshape(n_s, w).astype(BF16)
    sel = sel_rows[:, :, :MOBA_TOPK].reshape(db, MOBA_HEADS, t_new, MOBA_TOPK)

    q_heads = qa_s.reshape(db, t_new, MOBA_HEADS, HEAD_DIM).transpose(0, 2, 1, 3)
    d_own = np.arange(8)[None, :] - np.arange(t_new)[:, None]
    d_own = np.where(np.arange(8)[None, :] < t_new, d_own, -1)
    bown = _bias_from_dist(tab_a, d_own)[..., None]
    d_near = (past + np.arange(8))[:, None] - (past - page + np.arange(page))[None, :]
    bnear = _bias_from_dist(tab_a, d_near)
    cfar_s = jnp.broadcast_to(far_a[:, None, None], (MOBA_HEADS, 1, page))
    o_heads = _moba_sample(page_table, sel, pad_to(q_heads, 2, 8), pad_to(ka_s, 2, 8), pad_to(va_s, 2, 8),
                           bown, bnear, cfar_s, pool_ka_t, pool_va_t)
    o_a_s = o_heads[:, :, :t_new].transpose(0, 2, 1, 3).reshape(n_s, w).astype(BF16)
    y_sample = _merge_ffn(xs, o_a_s, o_b_s, merge_weights, n_s).reshape(db, t_new, d)

    a_shape = (MOBA_HEADS, HEAD_DIM)
    b_shape = (DIFF_HEADS, 2 * HEAD_DIM)
    rows_first = lambda a: a.reshape((batch,) + a_shape + (seq,)).transpose(0, 3, 1, 2)[None]
    return (y_prompt, y_sample,
            rows_first(ka_t), rows_first(va_t),
            kb.reshape((1, batch, seq) + b_shape), vb.reshape((1, batch, seq) + b_shape),
            ka_s.transpose(0, 2, 1, 3)[None], va_s.transpose(0, 2, 1, 3)[None],
            kb_s.reshape((1, db, t_new) + b_shape), vb_s.reshape((1, db, t_new) + b_shape))
```

```python
import functools
import math

import numpy as np
import jax
import jax.numpy as jnp
from jax import lax
from jax.experimental import pallas as pl
from jax.experimental.pallas import tpu as pltpu

HEAD_DIM = 64
MOBA_HEADS = 8
DIFF_HEADS = 4
MOBA_BLOCK = 256
MOBA_TOPK = 3
REL_BUCKETS = 32
REL_MAX_DIST = 128
RMS_EPS = 1e-6
LAMBDA_INIT = 0.8 - 0.6 * math.exp(-0.3 * 0)

LANES = 128
ATT_WIDTH = 512
UNITS = ATT_WIDTH // LANES
TQ = MOBA_BLOCK
MERGE_TM = 512
MOBA_PAIRS = 4
PAGES_PER_STEP = 16
VMEM_LIMIT = 56 * 1024 * 1024
BF16_ROWS = 16
V_ROWS = LANES + BF16_ROWS
LOG2E = math.log2(math.e)

F32 = jnp.float32
BF16 = jnp.bfloat16
NEG_INF = float("-inf")


def _rel_bucket_np(dist):
    n = np.maximum(dist, 0)
    exact = REL_BUCKETS // 2
    nf = np.maximum(n, 1).astype(np.float64)
    large = exact + (np.log(nf / exact) / math.log(REL_MAX_DIST / exact)
                     * (REL_BUCKETS - exact)).astype(np.int64)
    return np.where(n < exact, n, np.minimum(large, REL_BUCKETS - 1)).astype(np.int32)


def _bias_from_dist(tab, dist, per_row=False):
    idx = jnp.asarray(_rel_bucket_np(dist))
    col = (lambda k: tab[:, k:k + 1]) if per_row else (
        lambda k: tab[:, k].reshape((-1,) + (1,) * dist.ndim))
    b = jnp.broadcast_to(col(0), dist.shape if per_row else (tab.shape[0],) + dist.shape)
    for k in range(1, REL_BUCKETS):
        b = jnp.where(idx == k, col(k), b)
    return jnp.where(jnp.asarray(dist >= 0), b, NEG_INF)


def _nt(a, b, **kw):
    return lax.dot_general(a, b, (((1,), (1,)), ((), ())), preferred_element_type=F32, **kw)


def _const_spec(shape):
    n = len(shape)
    return pl.BlockSpec(shape, lambda *_: (0,) * n, pipeline_mode=pl.Buffered(1))


def _with_ones_rows(vt):
    row = lax.broadcasted_iota(jnp.int32, (BF16_ROWS, vt.shape[1]), 0)
    ones = jnp.where(row == 0, 1.0, 0.0).astype(BF16)
    parts = []
    for u in range(UNITS):
        parts += [vt[u * LANES:(u + 1) * LANES].astype(BF16), ones]
    return jnp.concatenate(parts, axis=0)


def _proj_kernel(scale, x_ref, g_ref, w_ref, qk_ref, kat_ref, vat_ref, kb_ref, vb_ref, vat16_ref,
                 vbt16_ref, kmean_ref):
    x = x_ref[...]
    xn = x * lax.rsqrt(jnp.mean(x * x, axis=-1, keepdims=True) + RMS_EPS) * g_ref[...]
    p = jnp.dot(xn.astype(BF16), w_ref[...], preferred_element_type=F32)
    w = ATT_WIDTH
    ka, va = p[:, w:2 * w], p[:, 2 * w:3 * w]
    kb, vb = p[:, 4 * w:5 * w], p[:, 5 * w:6 * w]
    vat = va.T
    kat_ref[0] = ka.T
    vat_ref[0] = vat
    for h in range(DIFF_HEADS):
        kb_ref[:, h, :] = kb[:, h * LANES:(h + 1) * LANES]
        vb_ref[:, h, :] = vb[:, h * LANES:(h + 1) * LANES]
    qk_ref[:, 0:w] = (p[:, 0:w] * scale).astype(BF16)
    qk_ref[:, w:2 * w] = ka.astype(BF16)
    qk_ref[:, 2 * w:3 * w] = (p[:, 3 * w:4 * w] * scale).astype(BF16)
    qk_ref[:, 3 * w:4 * w] = kb.astype(BF16)
    vat16_ref[0] = _with_ones_rows(vat)
    vbt16_ref[0] = _with_ones_rows(vb.T)
    kmean_ref[0] = jnp.mean(ka, axis=0, keepdims=True)


def _project(x2d, g, w_bf16, tm, tiles_per_seq, q_scale):
    n, d = x2d.shape
    w = ATT_WIDTH
    steps = n // tm
    seq = tm * tiles_per_seq
    row = lambda i: (i, 0)
    f32_rows = jax.ShapeDtypeStruct((n, DIFF_HEADS, 2 * HEAD_DIM), F32)
    rows_spec = pl.BlockSpec((tm, DIFF_HEADS, 2 * HEAD_DIM), lambda i: (i, 0, 0))
    f32_t = jax.ShapeDtypeStruct((n // seq, w, seq), F32)
    t_spec = pl.BlockSpec((1, w, tm), lambda i: (i // tiles_per_seq, 0, i % tiles_per_seq))
    vt16 = jax.ShapeDtypeStruct((steps, UNITS * V_ROWS, tm), BF16)
    return pl.pallas_call(
        functools.partial(_proj_kernel, q_scale),
        grid=(steps,),
        in_specs=[pl.BlockSpec((tm, d), row), _const_spec((1, d)), _const_spec((d, 6 * w))],
        out_specs=[pl.BlockSpec((tm, 4 * w), row), t_spec, t_spec,
                   rows_spec, rows_spec]
                  + [pl.BlockSpec((1, UNITS * V_ROWS, tm), lambda i: (i, 0, 0))] * 2
                  + [pl.BlockSpec((1, 1, w), lambda i: (i, 0, 0))],
        out_shape=[jax.ShapeDtypeStruct((n, 4 * w), BF16), f32_t, f32_t, f32_rows, f32_rows,
                   vt16, vt16, jax.ShapeDtypeStruct((steps, 1, w), F32)],
        compiler_params=pltpu.CompilerParams(dimension_semantics=("arbitrary",),
                                             vmem_limit_bytes=VMEM_LIMIT),
    )(x2d, g.reshape(1, d), w_bf16)


def _split_halves(q2):
    lane = lax.broadcasted_iota(jnp.int32, q2.shape, 1)
    zero = jnp.zeros_like(q2)
    return jnp.concatenate([jnp.where(lane < HEAD_DIM, q2, zero),
                            jnp.where(lane >= HEAD_DIM, q2, zero)], axis=0)


def _online_update(u, s, shift, vt, m_ref, acc_ref):
    m_old = m_ref[u]
    m_new = jnp.maximum(m_old, jnp.max(s, axis=0, keepdims=True) + shift)
    p = jnp.exp2(s - (m_new - shift))
    alpha = jnp.exp2(m_old - m_new)
    acc_ref[u] = alpha * acc_ref[u] + jnp.dot(vt, p.astype(BF16), preferred_element_type=F32)
    m_ref[u] = m_new


def _units_ahead(score_fn, fold_fn, first=None, after=None):
    s_next = score_fn(0) if first is None else first
    for u in range(UNITS):
        s = s_next
        if u + 1 < UNITS:
            s_next = score_fn(u + 1)
        elif after is not None:
            s_next = after()
        fold_fn(u, s)
    return s_next


def _attn_kernel(moba, *refs):
    if moba:
        (q_ref, k_ref, vt_ref, bias_ref, cfar_ref, kmean_ref, o_ref,
         qs_ref, m_ref, acc_ref, sh_ref) = refs
    else:
        (q_ref, k_ref, vt_ref, bias_ref, cfar_ref, lam_ref, subln_ref, o_ref,
         qs_ref, m_ref, acc_ref) = refs
    i = pl.program_id(1)
    tq = TQ

    def k_tile(n, u):
        return k_ref[pl.ds(pl.multiple_of(n * tq, tq), tq), u * LANES:(u + 1) * LANES]

    def vt_tile(n, u):
        return vt_ref[n, u * V_ROWS:(u + 1) * V_ROWS, :]

    def scores(n):
        return lambda u: _nt(k_tile(n, u), qs_ref[u])

    for u in range(UNITS):
        qs = _split_halves(q_ref[:, u * LANES:(u + 1) * LANES])
        qs_ref[u] = qs
        if moba:
            km = kmean_ref[0, :, u * LANES:(u + 1) * LANES]
            hi = km.astype(BF16)
            r1 = km - hi.astype(F32)
            mid = r1.astype(BF16)
            lo = (r1 - mid.astype(F32)).astype(BF16)
            g = _nt(hi, qs) + _nt(mid, qs) + _nt(lo, qs)
            nb = g.shape[0]
            blk = lax.broadcasted_iota(jnp.int32, g.shape, 0)
            valid = blk < i
            g = jnp.where(valid, g, NEG_INF)
            rank = jnp.zeros(g.shape, jnp.int32)
            for mm in range(nb):
                gm = g[mm:mm + 1, :]
                ahead = (gm > g) | ((gm == g) & (mm < blk))
                rank = rank + ahead.astype(jnp.int32)
            sel = valid & (rank < MOBA_TOPK)
            sh_ref[u] = jnp.where(sel, cfar_ref[u], NEG_INF)

    def fold_own(u, s):
        s = s + bias_ref[u, 0]
        m = jnp.max(s, axis=0, keepdims=True)
        p = jnp.exp2(s - m)
        m_ref[u] = m
        acc_ref[u] = jnp.dot(vt_tile(i, u), p.astype(BF16), preferred_element_type=F32)

    _units_ahead(scores(i), fold_own)

    @pl.when(i >= 1)
    def _():
        def fold_prev(u, s):
            if moba:
                shift = sh_ref[u, pl.ds(i - 1, 1), :] - cfar_ref[u]
            else:
                shift = jnp.zeros((1, 2 * tq), F32)
            _online_update(u, s + bias_ref[u, 1], shift, vt_tile(i - 1, u), m_ref, acc_ref)

        _units_ahead(scores(i - 1), fold_prev)

    def far(n, s_first):
        def fold_far(u, s):
            shift = sh_ref[u, pl.ds(n, 1), :] if moba else cfar_ref[u]
            _online_update(u, s, shift, vt_tile(n, u), m_ref, acc_ref)

        nxt = jnp.minimum(n + 1, i - 2)
        return _units_ahead(scores(n), fold_far, first=s_first, after=lambda: scores(nxt)(0))

    lax.fori_loop(0, i - 1, far, scores(0)(0))

    chan = lax.broadcasted_iota(jnp.int32, (LANES, tq), 0)
    if not moba:
        lp = lam_ref[...]
        lam = (jnp.exp(jnp.sum(lp[0:1] * lp[1:2], axis=-1, keepdims=True))
               - jnp.exp(jnp.sum(lp[2:3] * lp[3:4], axis=-1, keepdims=True)) + LAMBDA_INIT)
    for u in range(UNITS):
        acc = acc_ref[u]
        o = acc[:LANES] / acc[LANES:LANES + 1]
        if moba:
            out = jnp.where(chan < HEAD_DIM, o[:, :tq], o[:, tq:])
        else:
            out = o[:, :tq] - lam * o[:, tq:]
            out = out * lax.rsqrt(jnp.mean(out * out, axis=0, keepdims=True) + RMS_EPS)
            out = out * subln_ref[...] * (1.0 - LAMBDA_INIT)
        o_ref[:, u * LANES:(u + 1) * LANES] = out.T.astype(o_ref.dtype)


def _prompt_attention(moba, qk, vt, col0, batch, seq, bias, cfar, extra):
    nt = seq // TQ
    w = ATT_WIDTH
    in_specs = [
        pl.BlockSpec((TQ, w), lambda b, i: (b * nt + i, col0)),
        pl.BlockSpec((seq, w), lambda b, i: (b, col0 + 1)),
        pl.BlockSpec((nt, UNITS * V_ROWS, TQ), lambda b, i: (b, 0, 0)),
        _const_spec(bias.shape),
        _const_spec(cfar.shape),
    ]
    scratch = [pltpu.VMEM((UNITS, 2 * TQ, LANES), BF16),
               pltpu.VMEM((UNITS, 1, 2 * TQ), F32),
               pltpu.VMEM((UNITS, V_ROWS, 2 * TQ), F32)]
    if moba:
        (kmean,) = extra
        in_specs.append(pl.BlockSpec((1, nt, w), lambda b, i: (b, 0, 0)))
        scratch.append(pltpu.VMEM((UNITS, nt, 2 * TQ), F32))
    else:
        lam_params, subln_col = extra
        in_specs += [_const_spec(lam_params.shape), _const_spec(subln_col.shape)]
    return pl.pallas_call(
        functools.partial(_attn_kernel, moba),
        grid=(batch, nt),
        in_specs=in_specs,
        out_specs=pl.BlockSpec((TQ, w), lambda b, i: (b * nt + i, 0)),
        out_shape=jax.ShapeDtypeStruct((batch * seq, w), BF16),
        scratch_shapes=scratch,
        compiler_params=pltpu.CompilerParams(dimension_semantics=("arbitrary", "arbitrary"),
                                             vmem_limit_bytes=VMEM_LIMIT),
    )(qk, qk, vt, bias, cfar, *extra)


def _prompt_bias(tab, heads_of_unit):
    r = np.arange(TQ)[None, :] - np.arange(TQ)[:, None]
    own = _bias_from_dist(tab, r)
    prev = _bias_from_dist(tab, r + TQ)
    far = tab[:, REL_BUCKETS - 1]
    bias, cfar = [], []
    for h0, h1 in heads_of_unit:
        bias.append(jnp.stack([jnp.concatenate([own[h0], own[h1]], axis=1),
                               jnp.concatenate([prev[h0], prev[h1]], axis=1)]))
        cfar.append(jnp.concatenate([jnp.full((1, TQ), far[h0]), jnp.full((1, TQ), far[h1])], axis=1))
    return jnp.stack(bias) * LOG2E, jnp.stack(cfar) * LOG2E


def _merge_ffn_kernel(ffn_chunk, x_ref, oa_ref, ob_ref, gmix_ref, wg_ref, bg_ref, wa_ref, wb_ref,
                      wo_ref, gffn_ref, wup_ref, wdn_ref, gfin_ref, y_ref):
    x = x_ref[...]
    d = x.shape[1]
    xn = (x * lax.rsqrt(jnp.mean(x * x, axis=-1, keepdims=True) + RMS_EPS) * gmix_ref[...]).astype(BF16)
    gates = jax.nn.sigmoid(jnp.dot(xn, wg_ref[...], preferred_element_type=F32) + bg_ref[...])
    merged = (gates[:, :d] * jnp.dot(oa_ref[...], wa_ref[...], preferred_element_type=F32)
              + gates[:, d:] * jnp.dot(ob_ref[...], wb_ref[...], preferred_element_type=F32))
    h = x + jnp.dot(merged.astype(BF16), wo_ref[...], preferred_element_type=F32)
    hn = (h * lax.rsqrt(jnp.mean(h * h, axis=-1, keepdims=True) + RMS_EPS) * gffn_ref[...]).astype(BF16)
    hidden = wdn_ref.shape[0]
    out = h
    for c in range(hidden // ffn_chunk):
        lo = c * ffn_chunk
        gate = jnp.dot(hn, wup_ref[:, lo:lo + ffn_chunk], preferred_element_type=F32)
        up = jnp.dot(hn, wup_ref[:, hidden + lo:hidden + lo + ffn_chunk], preferred_element_type=F32)
        act = (gate * jax.nn.sigmoid(gate) * up).astype(BF16)
        out = out + jnp.dot(act, wdn_ref[lo:lo + ffn_chunk, :], preferred_element_type=F32)
    y = out * lax.rsqrt(jnp.mean(out * out, axis=-1, keepdims=True) + RMS_EPS) * gfin_ref[...]
    y_ref[...] = y


def _merge_ffn(x2d, oa, ob, weights, tm):
    n, d = x2d.shape
    row = lambda i: (i, 0)
    in_specs = [pl.BlockSpec((tm, d), row), pl.BlockSpec((tm, ATT_WIDTH), row),
                pl.BlockSpec((tm, ATT_WIDTH), row)] + [_const_spec(w.shape) for w in weights]
    hidden = weights[-2].shape[0]
    return pl.pallas_call(
        functools.partial(_merge_ffn_kernel, math.gcd(hidden, 2 * LANES)),
        grid=(n // tm,),
        in_specs=in_specs,
        out_specs=pl.BlockSpec((tm, d), row),
        out_shape=jax.ShapeDtypeStruct((n, d), F32),
        compiler_params=pltpu.CompilerParams(dimension_semantics=("arbitrary",),
                                             vmem_limit_bytes=VMEM_LIMIT),
    )(x2d, oa, ob, *weights)


def _top_lanes(g, count):
    lane = lax.broadcasted_iota(jnp.int32, g.shape, 1)
    out = jnp.zeros(g.shape, jnp.int32)
    for k in range(count):
        best = jnp.max(g, axis=-1, keepdims=True)
        idx = jnp.min(jnp.where(g == best, lane, LANES), axis=-1, keepdims=True)
        out = jnp.where(lane == k, idx, out)
        g = jnp.where(lane == idx, NEG_INF, g)
    return out


def _stream_kernel(ppb, pt_ref, qb_ref, qa_ref, knew_ref, vnew_ref, bnew_ref, bias_ref, lam_ref,
                   subln_ref, *refs):
    del pt_ref
    pps = PAGES_PER_STEP
    kb_refs, vb_refs, ka_refs = refs[:pps], refs[pps:2 * pps], refs[2 * pps:3 * pps]
    ob_ref, idx_ref, m_ref, l_ref, acc_ref, gate_ref = refs[3 * pps:]
    c = pl.program_id(1)
    nc = pl.num_programs(1)
    qb = qb_ref[0]
    half = qb.shape[0] // 2

    @pl.when(c == 0)
    def _():
        s = _nt(qb, knew_ref[0]) + bnew_ref[...]
        m = jnp.max(s, axis=-1, keepdims=True)
        p = jnp.exp(s - m)
        m_ref[...] = m
        l_ref[...] = jnp.sum(p, axis=-1, keepdims=True)
        acc_ref[...] = jnp.dot(p, vnew_ref[0], preferred_element_type=F32)
        gate_ref[...] = jnp.full(gate_ref.shape, NEG_INF, F32)

    s = jnp.concatenate([_nt(qb, kr[...]) for kr in kb_refs], axis=-1) + bias_ref[0]
    m_old = m_ref[...]
    m_new = jnp.maximum(m_old, jnp.max(s, axis=-1, keepdims=True))
    p = jnp.exp(s - m_new)
    alpha = jnp.exp(m_old - m_new)
    l_ref[...] = alpha * l_ref[...] + jnp.sum(p, axis=-1, keepdims=True)
    width = kb_refs[0].shape[0]
    pv = jnp.dot(p[:, :width], vb_refs[0][...], preferred_element_type=F32)
    for j in range(1, pps):
        pv = pv + jnp.dot(p[:, j * width:(j + 1) * width], vb_refs[j][...], preferred_element_type=F32)
    acc_ref[...] = alpha * acc_ref[...] + pv
    m_ref[...] = m_new

    qa = qa_ref[0]
    gate = gate_ref[...]
    lane = lax.broadcasted_iota(jnp.int32, gate.shape, 1)
    for j in range(pps // ppb):
        sa = jnp.dot(qa, ka_refs[j * ppb][...], preferred_element_type=F32)
        for jj in range(1, ppb):
            sa = sa + jnp.dot(qa, ka_refs[j * ppb + jj][...], preferred_element_type=F32)
        g = jnp.sum(sa, axis=-1, keepdims=True)
        gate = jnp.where(lane == c * (pps // ppb) + j, g, gate)
    gate_ref[...] = gate

    @pl.when(c == nc - 1)
    def _():
        lp = lam_ref[...]
        lam = (jnp.exp(jnp.sum(lp[0:1] * lp[1:2], axis=-1, keepdims=True))
               - jnp.exp(jnp.sum(lp[2:3] * lp[3:4], axis=-1, keepdims=True)) + LAMBDA_INIT)
        o = acc_ref[...] / l_ref[...]
        out = o[:half] - lam * o[half:]
        out = out * lax.rsqrt(jnp.mean(out * out, axis=-1, keepdims=True) + RMS_EPS)
        ob_ref[0] = out * subln_ref[...] * (1.0 - LAMBDA_INIT)
        idx_ref[0] = _top_lanes(gate, MOBA_TOPK)


def _sample_stream(page_table, qb_rows, qa_rows, knew, vnew, bnew, bias, lam_params, subln,
                   pool_kb, pool_vb, pool_ka_t, page):
    db, n_pages = page_table.shape
    pps = PAGES_PER_STEP
    nc = n_pages // pps
    rows_b, rows_a = qb_rows.shape[1], qa_rows.shape[1]
    blk_b, blk_a = pool_kb.shape[1:], pool_ka_t.shape[1:]

    def page_spec(j, blk):
        return pl.BlockSpec((None,) + blk, lambda b, c, pt: (pt[b * n_pages + c * pps + j], 0, 0))

    per_b = lambda b, c, pt: (b, 0, 0)
    fixed = lambda b, c, pt: (0, 0)
    in_specs = [
        pl.BlockSpec((1,) + qb_rows.shape[1:], per_b),
        pl.BlockSpec((1,) + qa_rows.shape[1:], per_b),
        pl.BlockSpec((1,) + knew.shape[1:], per_b),
        pl.BlockSpec((1,) + vnew.shape[1:], per_b),
        pl.BlockSpec(bnew.shape, fixed),
        pl.BlockSpec((1,) + bias.shape[1:], lambda b, c, pt: (jnp.where(c == nc - 1, 1, 0), 0, 0)),
        pl.BlockSpec(lam_params.shape, fixed),
        pl.BlockSpec(subln.shape, fixed),
    ] + [page_spec(j, blk_b) for j in range(pps)] * 2 + [page_spec(j, blk_a) for j in range(pps)]
    grid_spec = pltpu.PrefetchScalarGridSpec(
        num_scalar_prefetch=1,
        grid=(db, nc),
        in_specs=in_specs,
        out_specs=[pl.BlockSpec((1, rows_b // 2, LANES), per_b), pl.BlockSpec((1, rows_a, LANES), per_b)],
        scratch_shapes=[pltpu.VMEM((rows_b, 1), F32), pltpu.VMEM((rows_b, 1), F32),
                        pltpu.VMEM((rows_b, LANES), F32), pltpu.VMEM((rows_a, LANES), F32)],
    )
    return pl.pallas_call(
        functools.partial(_stream_kernel, MOBA_BLOCK // page),
        grid_spec=grid_spec,
        out_shape=[jax.ShapeDtypeStruct((db, rows_b // 2, LANES), F32),
                   jax.ShapeDtypeStruct((db, rows_a, LANES), jnp.int32)],
        compiler_params=pltpu.CompilerParams(dimension_semantics=("arbitrary", "arbitrary"),
                                             vmem_limit_bytes=VMEM_LIMIT),
    )(page_table.reshape(-1), qb_rows, qa_rows, knew, vnew, bnew, bias, lam_params, subln,
      *([pool_kb] * pps), *([pool_vb] * pps), *([pool_ka_t] * pps))


def _moba_sample_kernel(ppb, last_block, t_new, pages_per_req, pt_ref, sel_ref, q_ref, knew_ref,
                        vnew_ref, bown_ref, bnear_ref, cfar_ref, pool_k_ref, pool_v_ref, o_ref,
                        kbuf, vbuf, sem):
    n_sel = ppb * MOBA_TOPK
    n_cp = t_new * n_sel
    n_heads = q_ref.shape[1]
    n_it = q_ref.shape[0] * n_heads // MOBA_PAIRS
    page = kbuf.shape[3]

    def page_copy(pool_ref, buf, kind, slot, jt, phys, h):
        return pltpu.make_async_copy(pool_ref.at[phys, h], buf.at[slot, jt], sem.at[slot, kind, jt])

    def start_all(it, slot):
        for pr in range(MOBA_PAIRS):
            g = it * MOBA_PAIRS + pr
            b, h = g // n_heads, g % n_heads
            for jt in range(n_cp):
                t, k, p = jt // n_sel, (jt % n_sel) // ppb, jt % ppb
                blk = sel_ref[(g * t_new + t) * MOBA_TOPK + k]
                phys = pt_ref[b * pages_per_req + blk * ppb + p]
                page_copy(pool_k_ref, kbuf, 0, slot, pr * n_cp + jt, phys, h).start()
                page_copy(pool_v_ref, vbuf, 1, slot, pr * n_cp + jt, phys, h).start()

    def wait_all(slot):
        for jt in range(MOBA_PAIRS * n_cp):
            page_copy(pool_k_ref, kbuf, 0, slot, jt, 0, 0).wait()
            page_copy(pool_v_ref, vbuf, 1, slot, jt, 0, 0).wait()

    row = lax.broadcasted_iota(jnp.int32, (8, page), 0)
    row_d = lax.broadcasted_iota(jnp.int32, (8, HEAD_DIM), 0)

    def score_phase(g, slot, base):
        b, h = g // n_heads, g % n_heads
        q8 = q_ref[b, h]
        bnear = bnear_ref[h]
        scores = []
        for j in range(n_sel):
            s_j = jnp.zeros((8, page), F32)
            for t in range(t_new):
                res = jnp.dot(q8, kbuf[slot, base + t * n_sel + j], preferred_element_type=F32)
                s_j = jnp.where(row == t, res, s_j)
            bias = jnp.broadcast_to(cfar_ref[h], (8, page))
            if j % ppb == ppb - 1:
                for t in range(t_new):
                    blk = sel_ref[(g * t_new + t) * MOBA_TOPK + j // ppb]
                    bias = jnp.where((row == t) & (blk == last_block), bnear, bias)
            scores.append(s_j + bias)
        s = jnp.concatenate(scores, axis=-1)
        knew = knew_ref[b, h]
        own = [jnp.sum(q8 * knew[t2:t2 + 1, :], axis=-1, keepdims=True) + bown_ref[h, t2]
               for t2 in range(t_new)]
        return b, h, s, own

    def softmax_phase(b, h, s, own):
        vnew = vnew_ref[b, h]
        m = jnp.max(s, axis=-1, keepdims=True)
        for c in own:
            m = jnp.maximum(m, c)
        p = jnp.exp(s - m)
        l = jnp.sum(p, axis=-1, keepdims=True)
        acc = jnp.zeros((8, HEAD_DIM), F32)
        for t2, c in enumerate(own):
            p_own = jnp.exp(c - m)
            l = l + p_own
            acc = acc + p_own * vnew[t2:t2 + 1, :]
        return p, l, acc

    def value_phase(p, l, acc, slot, base):
        for j in range(n_sel):
            for t in range(t_new):
                res = _nt(p[:, j * page:(j + 1) * page], vbuf[slot, base + t * n_sel + j])
                acc = acc + jnp.where(row_d == t, res, 0.0)
        return jnp.where(row_d < t_new, acc / l, 0.0)

    def body(it, carry):
        slot = it % 2
        wait_all(slot)
        pairs = range(MOBA_PAIRS)
        scored = [score_phase(it * MOBA_PAIRS + pr, slot, pr * n_cp) for pr in pairs]
        start_all(jnp.minimum(it + 1, n_it - 1), 1 - slot)
        soft = [softmax_phase(*sc) for sc in scored]
        outs = [value_phase(*soft[pr], slot, pr * n_cp) for pr in pairs]
        for (b, h, _, _), out in zip(scored, outs):
            o_ref[b, h] = out
        return carry

    start_all(0, 0)
    lax.fori_loop(0, n_it, body, 0)
    wait_all(n_it % 2)


def _moba_sample(page_table, sel, q8, knew8, vnew8, bown, bnear, cfar, pool_k_t, pool_v_t):
    db, n_pages = page_table.shape
    page = pool_k_t.shape[3]
    t_new = sel.shape[2]
    ppb = MOBA_BLOCK // page
    last_block = n_pages // ppb - 1
    n_cp = MOBA_PAIRS * t_new * MOBA_TOPK * ppb
    assert (db * MOBA_HEADS) % MOBA_PAIRS == 0
    whole = lambda a: pl.BlockSpec(a.shape, lambda i, pt, sl: (0,) * a.ndim)
    hbm = pl.BlockSpec(memory_space=pl.ANY)
    out_shape = jax.ShapeDtypeStruct((db, MOBA_HEADS, 8, HEAD_DIM), F32)
    grid_spec = pltpu.PrefetchScalarGridSpec(
        num_scalar_prefetch=2,
        grid=(1,),
        in_specs=[whole(q8), whole(knew8), whole(vnew8), whole(bown), whole(bnear), whole(cfar), hbm, hbm],
        out_specs=pl.BlockSpec(out_shape.shape, lambda i, pt, sl: (0, 0, 0, 0)),
        scratch_shapes=[pltpu.VMEM((2, n_cp, HEAD_DIM, page), F32),
                        pltpu.VMEM((2, n_cp, HEAD_DIM, page), F32),
                        pltpu.SemaphoreType.DMA((2, 2, n_cp))],
    )
    return pl.pallas_call(
        functools.partial(_moba_sample_kernel, ppb, last_block, t_new, n_pages),
        grid_spec=grid_spec,
        out_shape=out_shape,
        compiler_params=pltpu.CompilerParams(dimension_semantics=("arbitrary",),
                                             vmem_limit_bytes=VMEM_LIMIT),
    )(page_table.reshape(-1), sel.reshape(-1), q8, knew8, vnew8, bown, bnear, cfar, pool_k_t, pool_v_t)


def kernel(x_prompt, x_sample, cache_k_a, cache_v_a, cache_k_b, cache_v_b, page_table, g_mix, w_in,
           w_merge_gate, b_merge_gate, w_branch_a, w_branch_b, w_out, lambda_q1, lambda_k1, lambda_q2,
           lambda_k2, subln_w, g_ffn, w_ffn_up, w_ffn_down, rel_bias, g_final):
    batch, seq, d = x_prompt.shape
    db, t_new, _ = x_sample.shape
    depth, n_pool, page = cache_k_a.shape[:3]
    n_pages = page_table.shape[1]
    past = n_pages * page
    nb_past = past // MOBA_BLOCK
    w = ATT_WIDTH
    assert depth == 1 and seq % TQ == 0 and past % MOBA_BLOCK == 0 and MOBA_BLOCK % page == 0
    assert n_pages % PAGES_PER_STEP == 0 and MOBA_TOPK <= nb_past <= LANES and t_new <= 8
    assert page == LANES and PAGES_PER_STEP % (MOBA_BLOCK // page) == 0

    tab = rel_bias.astype(F32).T
    tab_a, tab_b = tab[:MOBA_HEADS], tab[MOBA_HEADS:]
    far_a, far_b = tab_a[:, REL_BUCKETS - 1], tab_b[:, REL_BUCKETS - 1]
    lam_params = jnp.concatenate([lambda_q1, lambda_k1, lambda_q2, lambda_k2], axis=0).astype(F32)
    subln = subln_w.astype(F32).reshape(1, 2 * HEAD_DIM)
    w_in_b = w_in[0].astype(BF16)
    merge_weights = (g_mix.reshape(1, d), w_merge_gate[0].astype(BF16), b_merge_gate.reshape(1, 2 * d),
                     w_branch_a[0].astype(BF16), w_branch_b[0].astype(BF16), w_out[0].astype(BF16),
                     g_ffn.reshape(1, d), w_ffn_up[0].astype(BF16), w_ffn_down[0].astype(BF16),
                     g_final.reshape(1, d))

    xp = x_prompt.reshape(batch * seq, d)
    scale = HEAD_DIM ** -0.5
    qk, ka_t, va_t, kb, vb, vat, vbt, kmean = _project(xp, g_mix, w_in_b, TQ, seq // TQ, scale * LOG2E)
    bias_a, cfar_a = _prompt_bias(tab_a, [(2 * u, 2 * u + 1) for u in range(UNITS)])
    bias_b, cfar_b = _prompt_bias(tab_b, [(u, u) for u in range(UNITS)])
    o_a = _prompt_attention(True, qk, vat, 0, batch, seq, bias_a, cfar_a,
                            (kmean.reshape(batch, seq // TQ, w),))
    o_b = _prompt_attention(False, qk, vbt, 2, batch, seq, bias_b, cfar_b,
                            (lam_params, subln.reshape(2 * HEAD_DIM, 1)))
    y_prompt = _merge_ffn(xp, o_a, o_b, merge_weights, MERGE_TM).reshape(batch, seq, d)

    n_s = db * t_new
    xs = x_sample.reshape(n_s, d)
    qk_s, ka_st, va_st, kb_s, vb_s, _, _, _ = _project(xs, g_mix, w_in_b, n_s, 1, scale)
    heads_first = lambda a: a.reshape(MOBA_HEADS, HEAD_DIM, db, t_new).transpose(2, 0, 3, 1)
    ka_s, va_s = heads_first(ka_st), heads_first(va_st)
    qk_s3 = qk_s.reshape(db, t_new, 4 * w).astype(F32)
    pad_to = lambda a, axis, n: jnp.pad(a, [(0, n - a.shape[i] if i == axis else 0) for i in range(a.ndim)])
    hb = DIFF_HEADS
    pool_kb = cache_k_b.reshape(n_pool, page * hb, 2 * HEAD_DIM)
    pool_vb = cache_v_b.reshape(n_pool, page * hb, 2 * HEAD_DIM)
    pool_ka_t = jnp.transpose(cache_k_a[0], (0, 2, 3, 1))
    pool_va_t = jnp.transpose(cache_v_a[0], (0, 2, 3, 1))

    lane = np.arange(LANES)
    qb16 = qk_s3[:, :, 2 * w:3 * w].reshape(db, t_new, hb, LANES).transpose(0, 2, 1, 3)
    qb16 = qb16.reshape(db, hb * t_new, LANES)
    qb_rows = jnp.concatenate([jnp.where(lane < HEAD_DIM, qb16, 0.0),
                               jnp.where(lane >= HEAD_DIM, qb16, 0.0)], axis=1)
    rows_b = 2 * hb * t_new
    head_of_row = (np.arange(rows_b) // t_new) % hb
    tok_of_row = np.arange(rows_b) % t_new
    head_mask = lambda cols: jnp.asarray(head_of_row[:, None] == (cols % hb)[None, :])
    tab_rows = tab_b[head_of_row]
    cols_new = np.arange(LANES)
    d_new = np.where(cols_new[None, :] < t_new * hb, tok_of_row[:, None] - (cols_new // hb)[None, :], -1)
    bnew = jnp.where(head_mask(cols_new), _bias_from_dist(tab_rows, d_new, per_row=True), NEG_INF)
    span = PAGES_PER_STEP * page
    cols = np.arange(span * hb)
    d_last = (past + tok_of_row)[:, None] - (past - span + cols // hb)[None, :]
    bias_s = jnp.stack([jnp.broadcast_to(far_b[head_of_row][:, None], (rows_b, span * hb)),
                        _bias_from_dist(tab_rows, d_last, per_row=True)])
    bias_s = jnp.where(head_mask(cols)[None], bias_s, NEG_INF)
    seg = np.arange(w) // HEAD_DIM
    qa_s = qk_s3[:, :, 0:w]
    qa_rows = jnp.where(jnp.asarray(seg[None, None, :] == np.arange(MOBA_HEADS)[:, None, None]),
                        qa_s[:, None], 0.0).reshape(db, MOBA_HEADS * t_new, w)
    knew_b = pad_to(kb_s.reshape(db, t_new * hb, 2 * HEAD_DIM), 1, LANES)
    vnew_b = pad_to(vb_s.reshape(db, t_new * hb, 2 * HEAD_DIM), 1, LANES)
    o_b16, sel_rows = _sample_stream(
        page_table, qb_rows, qa_rows, knew_b, vnew_b, bnew, bias_s, lam_params, subln,
        pool_kb, pool_vb, pool_ka_t.reshape(n_pool, w, page), page)
    o_b_s = o_b16.reshape(db, hb, t_new, LANES).transpose(0, 2, 1, 3).reshape(n_s, w).astype(BF16)
    sel = sel_rows[:, :, :MOBA_TOPK].reshape(db, MOBA_HEADS, t_new, MOBA_TOPK)

    q_heads = qa_s.reshape(db, t_new, MOBA_HEADS, HEAD_DIM).transpose(0, 2, 1, 3)
    d_own = np.arange(8)[None, :] - np.arange(t_new)[:, None]
    d_own = np.where(np.arange(8)[None, :] < t_new, d_own, -1)
    bown = _bias_from_dist(tab_a, d_own)[..., None]
    d_near = (past + np.arange(8))[:, None] - (past - page + np.arange(page))[None, :]
    bnear = _bias_from_dist(tab_a, d_near)
    cfar_s = jnp.broadcast_to(far_a[:, None, None], (MOBA_HEADS, 1, page))
    o_heads = _moba_sample(page_table, sel, pad_to(q_heads, 2, 8), pad_to(ka_s, 2, 8), pad_to(va_s, 2, 8),
                           bown, bnear, cfar_s, pool_ka_t, pool_va_t)
    o_a_s = o_heads[:, :, :t_new].transpose(0, 2, 1, 3).reshape(n_s, w).astype(BF16)
    y_sample = _merge_ffn(xs, o_a_s, o_b_s, merge_weights, n_s).reshape(db, t_new, d)

    a_shape = (MOBA_HEADS, HEAD_DIM)
    b_shape = (DIFF_HEADS, 2 * HEAD_DIM)
    rows_first = lambda a: a.reshape((batch,) + a_shape + (seq,)).transpose(0, 3, 1, 2)[None]
    return (y_prompt, y_sample,
            rows_first(ka_t), rows_first(va_t),
            kb.reshape((1, batch, seq) + b_shape), vb.reshape((1, batch, seq) + b_shape),
            ka_s.transpose(0, 2, 1, 3)[None], va_s.transpose(0, 2, 1, 3)[None],
            kb_s.reshape((1, db, t_new) + b_shape), vb_s.reshape((1, db, t_new) + b_shape))
```

```python
import functools
import math

import numpy as np
import jax
import jax.numpy as jnp
from jax import lax
from jax.experimental import pallas as pl
from jax.experimental.pallas import tpu as pltpu

HEAD_DIM = 64
MOBA_HEADS = 8
DIFF_HEADS = 4
MOBA_BLOCK = 256
MOBA_TOPK = 3
REL_BUCKETS = 32
REL_MAX_DIST = 128
RMS_EPS = 1e-6
LAMBDA_INIT = 0.8 - 0.6 * math.exp(-0.3 * 0)

LANES = 128
ATT_WIDTH = 512
UNITS = ATT_WIDTH // LANES
TQ = MOBA_BLOCK
SCORE_AHEAD = 2
MERGE_TM = 512
MOBA_PAIRS = 4
PAGES_PER_STEP = 16
VMEM_LIMIT = 56 * 1024 * 1024
BF16_ROWS = 16
V_ROWS = LANES + BF16_ROWS
LOG2E = math.log2(math.e)

F32 = jnp.float32
BF16 = jnp.bfloat16
NEG_INF = float("-inf")


def _rel_bucket_np(dist):
    n = np.maximum(dist, 0)
    exact = REL_BUCKETS // 2
    nf = np.maximum(n, 1).astype(np.float64)
    large = exact + (np.log(nf / exact) / math.log(REL_MAX_DIST / exact)
                     * (REL_BUCKETS - exact)).astype(np.int64)
    return np.where(n < exact, n, np.minimum(large, REL_BUCKETS - 1)).astype(np.int32)


def _bias_from_dist(tab, dist, per_row=False):
    idx = jnp.asarray(_rel_bucket_np(dist))
    col = (lambda k: tab[:, k:k + 1]) if per_row else (
        lambda k: tab[:, k].reshape((-1,) + (1,) * dist.ndim))
    b = jnp.broadcast_to(col(0), dist.shape if per_row else (tab.shape[0],) + dist.shape)
    for k in range(1, REL_BUCKETS):
        b = jnp.where(idx == k, col(k), b)
    return jnp.where(jnp.asarray(dist >= 0), b, NEG_INF)


def _nt(a, b, **kw):
    return lax.dot_general(a, b, (((1,), (1,)), ((), ())), preferred_element_type=F32, **kw)


def _const_spec(shape):
    n = len(shape)
    return pl.BlockSpec(shape, lambda *_: (0,) * n, pipeline_mode=pl.Buffered(1))


def _with_ones_rows(vt):
    row = lax.broadcasted_iota(jnp.int32, (BF16_ROWS, vt.shape[1]), 0)
    ones = jnp.where(row == 0, 1.0, 0.0).astype(BF16)
    parts = []
    for u in range(UNITS):
        parts += [vt[u * LANES:(u + 1) * LANES].astype(BF16), ones]
    return jnp.concatenate(parts, axis=0)


def _proj_kernel(scale, x_ref, g_ref, w_ref, qk_ref, kat_ref, vat_ref, kb_ref, vb_ref, vat16_ref,
                 vbt16_ref, kmean_ref):
    x = x_ref[...]
    xn = x * lax.rsqrt(jnp.mean(x * x, axis=-1, keepdims=True) + RMS_EPS) * g_ref[...]
    p = jnp.dot(xn.astype(BF16), w_ref[...], preferred_element_type=F32)
    w = ATT_WIDTH
    ka, va = p[:, w:2 * w], p[:, 2 * w:3 * w]
    kb, vb = p[:, 4 * w:5 * w], p[:, 5 * w:6 * w]
    vat = va.T
    kat_ref[0] = ka.T
    vat_ref[0] = vat
    for h in range(DIFF_HEADS):
        kb_ref[:, h, :] = kb[:, h * LANES:(h + 1) * LANES]
        vb_ref[:, h, :] = vb[:, h * LANES:(h + 1) * LANES]
    qk_ref[:, 0:w] = (p[:, 0:w] * scale).astype(BF16)
    qk_ref[:, w:2 * w] = ka.astype(BF16)
    qk_ref[:, 2 * w:3 * w] = (p[:, 3 * w:4 * w] * scale).astype(BF16)
    qk_ref[:, 3 * w:4 * w] = kb.astype(BF16)
    vat16_ref[0] = _with_ones_rows(vat)
    vbt16_ref[0] = _with_ones_rows(vb.T)
    kmean_ref[0] = jnp.mean(ka, axis=0, keepdims=True)


def _project(x2d, g, w_bf16, tm, tiles_per_seq, q_scale):
    n, d = x2d.shape
    w = ATT_WIDTH
    steps = n // tm
    seq = tm * tiles_per_seq
    row = lambda i: (i, 0)
    f32_rows = jax.ShapeDtypeStruct((n, DIFF_HEADS, 2 * HEAD_DIM), F32)
    rows_spec = pl.BlockSpec((tm, DIFF_HEADS, 2 * HEAD_DIM), lambda i: (i, 0, 0))
    f32_t = jax.ShapeDtypeStruct((n // seq, w, seq), F32)
    t_spec = pl.BlockSpec((1, w, tm), lambda i: (i // tiles_per_seq, 0, i % tiles_per_seq))
    vt16 = jax.ShapeDtypeStruct((steps, UNITS * V_ROWS, tm), BF16)
    return pl.pallas_call(
        functools.partial(_proj_kernel, q_scale),
        grid=(steps,),
        in_specs=[pl.BlockSpec((tm, d), row), _const_spec((1, d)), _const_spec((d, 6 * w))],
        out_specs=[pl.BlockSpec((tm, 4 * w), row), t_spec, t_spec,
                   rows_spec, rows_spec]
                  + [pl.BlockSpec((1, UNITS * V_ROWS, tm), lambda i: (i, 0, 0))] * 2
                  + [pl.BlockSpec((1, 1, w), lambda i: (i, 0, 0))],
        out_shape=[jax.ShapeDtypeStruct((n, 4 * w), BF16), f32_t, f32_t, f32_rows, f32_rows,
                   vt16, vt16, jax.ShapeDtypeStruct((steps, 1, w), F32)],
        compiler_params=pltpu.CompilerParams(dimension_semantics=("arbitrary",),
                                             vmem_limit_bytes=VMEM_LIMIT),
    )(x2d, g.reshape(1, d), w_bf16)


def _split_halves(q2):
    lane = lax.broadcasted_iota(jnp.int32, q2.shape, 1)
    zero = jnp.zeros_like(q2)
    return jnp.concatenate([jnp.where(lane < HEAD_DIM, q2, zero),
                            jnp.where(lane >= HEAD_DIM, q2, zero)], axis=0)


def _online_update(u, s, shift, vt, m_ref, acc_ref):
    m_old = m_ref[u]
    m_new = jnp.maximum(m_old, jnp.max(s, axis=0, keepdims=True) + shift)
    p = jnp.exp2(s - (m_new - shift))
    alpha = jnp.exp2(m_old - m_new)
    acc_ref[u] = alpha * acc_ref[u] + jnp.dot(vt, p.astype(BF16), preferred_element_type=F32)
    m_ref[u] = m_new


def _units_ahead(score_fn, fold_fn, first=(), after=None):
    pending = list(first)
    issued = len(pending)
    total = UNITS + (SCORE_AHEAD if after is not None else 0)
    for u in range(UNITS):
        while issued < min(u + SCORE_AHEAD + 1, total):
            pending.append(score_fn(issued) if issued < UNITS else after(issued - UNITS))
            issued += 1
        fold_fn(u, pending.pop(0))
    return tuple(pending)


def _attn_kernel(moba, *refs):
    if moba:
        (q_ref, k_ref, vt_ref, bias_ref, cfar_ref, kmean_ref, o_ref,
         qs_ref, m_ref, acc_ref, sh_ref) = refs
    else:
        (q_ref, k_ref, vt_ref, bias_ref, cfar_ref, lam_ref, subln_ref, o_ref,
         qs_ref, m_ref, acc_ref) = refs
    i = pl.program_id(1)
    tq = TQ

    def k_tile(n, u):
        return k_ref[pl.ds(pl.multiple_of(n * tq, tq), tq), u * LANES:(u + 1) * LANES]

    def vt_tile(n, u):
        return vt_ref[n, u * V_ROWS:(u + 1) * V_ROWS, :]

    def scores(n):
        return lambda u: _nt(k_tile(n, u), qs_ref[u])

    for u in range(UNITS):
        qs = _split_halves(q_ref[:, u * LANES:(u + 1) * LANES])
        qs_ref[u] = qs
        if moba:
            km = kmean_ref[0, :, u * LANES:(u + 1) * LANES]
            hi = km.astype(BF16)
            r1 = km - hi.astype(F32)
            mid = r1.astype(BF16)
            lo = (r1 - mid.astype(F32)).astype(BF16)
            g = _nt(hi, qs) + _nt(mid, qs) + _nt(lo, qs)
            nb = g.shape[0]
            blk = lax.broadcasted_iota(jnp.int32, g.shape, 0)
            valid = blk < i
            g = jnp.where(valid, g, NEG_INF)
            rank = jnp.zeros(g.shape, jnp.int32)
            for mm in range(nb):
                gm = g[mm:mm + 1, :]
                ahead = (gm > g) | ((gm == g) & (mm < blk))
                rank = rank + ahead.astype(jnp.int32)
            sel = valid & (rank < MOBA_TOPK)
            sh_ref[u] = jnp.where(sel, cfar_ref[u], NEG_INF)

    def fold_own(u, s):
        s = s + bias_ref[u, 0]
        m = jnp.max(s, axis=0, keepdims=True)
        p = jnp.exp2(s - m)
        m_ref[u] = m
        acc_ref[u] = jnp.dot(vt_tile(i, u), p.astype(BF16), preferred_element_type=F32)

    _units_ahead(scores(i), fold_own)

    @pl.when(i >= 1)
    def _():
        def fold_prev(u, s):
            if moba:
                shift = sh_ref[u, pl.ds(i - 1, 1), :] - cfar_ref[u]
            else:
                shift = jnp.zeros((1, 2 * tq), F32)
            _online_update(u, s + bias_ref[u, 1], shift, vt_tile(i - 1, u), m_ref, acc_ref)

        _units_ahead(scores(i - 1), fold_prev)

    def far(n, s_first):
        def fold_far(u, s):
            shift = sh_ref[u, pl.ds(n, 1), :] if moba else cfar_ref[u]
            _online_update(u, s, shift, vt_tile(n, u), m_ref, acc_ref)

        nxt = jnp.minimum(n + 1, i - 2)
        return _units_ahead(scores(n), fold_far, first=s_first, after=scores(nxt))

    lax.fori_loop(0, i - 1, far, tuple(scores(0)(k) for k in range(SCORE_AHEAD)))

    chan = lax.broadcasted_iota(jnp.int32, (LANES, tq), 0)
    if not moba:
        lp = lam_ref[...]
        lam = (jnp.exp(jnp.sum(lp[0:1] * lp[1:2], axis=-1, keepdims=True))
               - jnp.exp(jnp.sum(lp[2:3] * lp[3:4], axis=-1, keepdims=True)) + LAMBDA_INIT)
    for u in range(UNITS):
        acc = acc_ref[u]
        o = acc[:LANES] / acc[LANES:LANES + 1]
        if moba:
            out = jnp.where(chan < HEAD_DIM, o[:, :tq], o[:, tq:])
        else:
            out = o[:, :tq] - lam * o[:, tq:]
            out = out * lax.rsqrt(jnp.mean(out * out, axis=0, keepdims=True) + RMS_EPS)
            out = out * subln_ref[...] * (1.0 - LAMBDA_INIT)
        o_ref[:, u * LANES:(u + 1) * LANES] = out.T.astype(o_ref.dtype)


def _prompt_attention(moba, qk, vt, col0, batch, seq, bias, cfar, extra):
    nt = seq // TQ
    w = ATT_WIDTH
    in_specs = [
        pl.BlockSpec((TQ, w), lambda b, i: (b * nt + i, col0)),
        pl.BlockSpec((seq, w), lambda b, i: (b, col0 + 1)),
        pl.BlockSpec((nt, UNITS * V_ROWS, TQ), lambda b, i: (b, 0, 0)),
        _const_spec(bias.shape),
        _const_spec(cfar.shape),
    ]
    scratch = [pltpu.VMEM((UNITS, 2 * TQ, LANES), BF16),
               pltpu.VMEM((UNITS, 1, 2 * TQ), F32),
               pltpu.VMEM((UNITS, V_ROWS, 2 * TQ), F32)]
    if moba:
        (kmean,) = extra
        in_specs.append(pl.BlockSpec((1, nt, w), lambda b, i: (b, 0, 0)))
        scratch.append(pltpu.VMEM((UNITS, nt, 2 * TQ), F32))
    else:
        lam_params, subln_col = extra
        in_specs += [_const_spec(lam_params.shape), _const_spec(subln_col.shape)]
    return pl.pallas_call(
        functools.partial(_attn_kernel, moba),
        grid=(batch, nt),
        in_specs=in_specs,
        out_specs=pl.BlockSpec((TQ, w), lambda b, i: (b * nt + i, 0)),
        out_shape=jax.ShapeDtypeStruct((batch * seq, w), BF16),
        scratch_shapes=scratch,
        compiler_params=pltpu.CompilerParams(dimension_semantics=("arbitrary", "arbitrary"),
                                             vmem_limit_bytes=VMEM_LIMIT),
    )(qk, qk, vt, bias, cfar, *extra)


def _prompt_bias(tab, heads_of_unit):
    r = np.arange(TQ)[None, :] - np.arange(TQ)[:, None]
    own = _bias_from_dist(tab, r)
    prev = _bias_from_dist(tab, r + TQ)
    far = tab[:, REL_BUCKETS - 1]
    bias, cfar = [], []
    for h0, h1 in heads_of_unit:
        bias.append(jnp.stack([jnp.concatenate([own[h0], own[h1]], axis=1),
                               jnp.concatenate([prev[h0], prev[h1]], axis=1)]))
        cfar.append(jnp.concatenate([jnp.full((1, TQ), far[h0]), jnp.full((1, TQ), far[h1])], axis=1))
    return jnp.stack(bias) * LOG2E, jnp.stack(cfar) * LOG2E


def _merge_ffn_kernel(ffn_chunk, x_ref, oa_ref, ob_ref, gmix_ref, wg_ref, bg_ref, wa_ref, wb_ref,
                      wo_ref, gffn_ref, wup_ref, wdn_ref, gfin_ref, y_ref):
    x = x_ref[...]
    d = x.shape[1]
    xn = (x * lax.rsqrt(jnp.mean(x * x, axis=-1, keepdims=True) + RMS_EPS) * gmix_ref[...]).astype(BF16)
    gates = jax.nn.sigmoid(jnp.dot(xn, wg_ref[...], preferred_element_type=F32) + bg_ref[...])
    merged = (gates[:, :d] * jnp.dot(oa_ref[...], wa_ref[...], preferred_element_type=F32)
              + gates[:, d:] * jnp.dot(ob_ref[...], wb_ref[...], preferred_element_type=F32))
    h = x + jnp.dot(merged.astype(BF16), wo_ref[...], preferred_element_type=F32)
    hn = (h * lax.rsqrt(jnp.mean(h * h, axis=-1, keepdims=True) + RMS_EPS) * gffn_ref[...]).astype(BF16)
    hidden = wdn_ref.shape[0]
    out = h
    for c in range(hidden // ffn_chunk):
        lo = c * ffn_chunk
        gate = jnp.dot(hn, wup_ref[:, lo:lo + ffn_chunk], preferred_element_type=F32)
        up = jnp.dot(hn, wup_ref[:, hidden + lo:hidden + lo + ffn_chunk], preferred_element_type=F32)
        act = (gate * jax.nn.sigmoid(gate) * up).astype(BF16)
        out = out + jnp.dot(act, wdn_ref[lo:lo + ffn_chunk, :], preferred_element_type=F32)
    y = out * lax.rsqrt(jnp.mean(out * out, axis=-1, keepdims=True) + RMS_EPS) * gfin_ref[...]
    y_ref[...] = y


def _merge_ffn(x2d, oa, ob, weights, tm):
    n, d = x2d.shape
    row = lambda i: (i, 0)
    in_specs = [pl.BlockSpec((tm, d), row), pl.BlockSpec((tm, ATT_WIDTH), row),
                pl.BlockSpec((tm, ATT_WIDTH), row)] + [_const_spec(w.shape) for w in weights]
    hidden = weights[-2].shape[0]
    return pl.pallas_call(
        functools.partial(_merge_ffn_kernel, math.gcd(hidden, 2 * LANES)),
        grid=(n // tm,),
        in_specs=in_specs,
        out_specs=pl.BlockSpec((tm, d), row),
        out_shape=jax.ShapeDtypeStruct((n, d), F32),
        compiler_params=pltpu.CompilerParams(dimension_semantics=("arbitrary",),
                                             vmem_limit_bytes=VMEM_LIMIT),
    )(x2d, oa, ob, *weights)


def _top_lanes(g, count):
    lane = lax.broadcasted_iota(jnp.int32, g.shape, 1)
    out = jnp.zeros(g.shape, jnp.int32)
    for k in range(count):
        best = jnp.max(g, axis=-1, keepdims=True)
        idx = jnp.min(jnp.where(g == best, lane, LANES), axis=-1, keepdims=True)
        out = jnp.where(lane == k, idx, out)
        g = jnp.where(lane == idx, NEG_INF, g)
    return out


def _stream_kernel(ppb, pt_ref, qb_ref, qa_ref, knew_ref, vnew_ref, bnew_ref, bias_ref, lam_ref,
                   subln_ref, *refs):
    del pt_ref
    pps = PAGES_PER_STEP
    kb_refs, vb_refs, ka_refs = refs[:pps], refs[pps:2 * pps], refs[2 * pps:3 * pps]
    ob_ref, idx_ref, m_ref, l_ref, acc_ref, gate_ref = refs[3 * pps:]
    c = pl.program_id(1)
    nc = pl.num_programs(1)
    qb = qb_ref[0]
    half = qb.shape[0] // 2

    @pl.when(c == 0)
    def _():
        s = _nt(qb, knew_ref[0]) + bnew_ref[...]
        m = jnp.max(s, axis=-1, keepdims=True)
        p = jnp.exp(s - m)
        m_ref[...] = m
        l_ref[...] = jnp.sum(p, axis=-1, keepdims=True)
        acc_ref[...] = jnp.dot(p, vnew_ref[0], preferred_element_type=F32)
        gate_ref[...] = jnp.full(gate_ref.shape, NEG_INF, F32)

    s = jnp.concatenate([_nt(qb, kr[...]) for kr in kb_refs], axis=-1) + bias_ref[0]
    m_old = m_ref[...]
    m_new = jnp.maximum(m_old, jnp.max(s, axis=-1, keepdims=True))
    p = jnp.exp(s - m_new)
    alpha = jnp.exp(m_old - m_new)
    l_ref[...] = alpha * l_ref[...] + jnp.sum(p, axis=-1, keepdims=True)
    width = kb_refs[0].shape[0]
    pv = jnp.dot(p[:, :width], vb_refs[0][...], preferred_element_type=F32)
    for j in range(1, pps):
        pv = pv + jnp.dot(p[:, j * width:(j + 1) * width], vb_refs[j][...], preferred_element_type=F32)
    acc_ref[...] = alpha * acc_ref[...] + pv
    m_ref[...] = m_new

    qa = qa_ref[0]
    gate = gate_ref[...]
    lane = lax.broadcasted_iota(jnp.int32, gate.shape, 1)
    for j in range(pps // ppb):
        sa = jnp.dot(qa, ka_refs[j * ppb][...], preferred_element_type=F32)
        for jj in range(1, ppb):
            sa = sa + jnp.dot(qa, ka_refs[j * ppb + jj][...], preferred_element_type=F32)
        g = jnp.sum(sa, axis=-1, keepdims=True)
        gate = jnp.where(lane == c * (pps // ppb) + j, g, gate)
    gate_ref[...] = gate

    @pl.when(c == nc - 1)
    def _():
        lp = lam_ref[...]
        lam = (jnp.exp(jnp.sum(lp[0:1] * lp[1:2], axis=-1, keepdims=True))
               - jnp.exp(jnp.sum(lp[2:3] * lp[3:4], axis=-1, keepdims=True)) + LAMBDA_INIT)
        o = acc_ref[...] / l_ref[...]
        out = o[:half] - lam * o[half:]
        out = out * lax.rsqrt(jnp.mean(out * out, axis=-1, keepdims=True) + RMS_EPS)
        ob_ref[0] = out * subln_ref[...] * (1.0 - LAMBDA_INIT)
        idx_ref[0] = _top_lanes(gate, MOBA_TOPK)


def _sample_stream(page_table, qb_rows, qa_rows, knew, vnew, bnew, bias, lam_params, subln,
                   pool_kb, pool_vb, pool_ka_t, page):
    db, n_pages = page_table.shape
    pps = PAGES_PER_STEP
    nc = n_pages // pps
    rows_b, rows_a = qb_rows.shape[1], qa_rows.shape[1]
    blk_b, blk_a = pool_kb.shape[1:], pool_ka_t.shape[1:]

    def page_spec(j, blk):
        return pl.BlockSpec((None,) + blk, lambda b, c, pt: (pt[b * n_pages + c * pps + j], 0, 0))

    per_b = lambda b, c, pt: (b, 0, 0)
    fixed = lambda b, c, pt: (0, 0)
    in_specs = [
        pl.BlockSpec((1,) + qb_rows.shape[1:], per_b),
        pl.BlockSpec((1,) + qa_rows.shape[1:], per_b),
        pl.BlockSpec((1,) + knew.shape[1:], per_b),
        pl.BlockSpec((1,) + vnew.shape[1:], per_b),
        pl.BlockSpec(bnew.shape, fixed),
        pl.BlockSpec((1,) + bias.shape[1:], lambda b, c, pt: (jnp.where(c == nc - 1, 1, 0), 0, 0)),
        pl.BlockSpec(lam_params.shape, fixed),
        pl.BlockSpec(subln.shape, fixed),
    ] + [page_spec(j, blk_b) for j in range(pps)] * 2 + [page_spec(j, blk_a) for j in range(pps)]
    grid_spec = pltpu.PrefetchScalarGridSpec(
        num_scalar_prefetch=1,
        grid=(db, nc),
        in_specs=in_specs,
        out_specs=[pl.BlockSpec((1, rows_b // 2, LANES), per_b), pl.BlockSpec((1, rows_a, LANES), per_b)],
        scratch_shapes=[pltpu.VMEM((rows_b, 1), F32), pltpu.VMEM((rows_b, 1), F32),
                        pltpu.VMEM((rows_b, LANES), F32), pltpu.VMEM((rows_a, LANES), F32)],
    )
    return pl.pallas_call(
        functools.partial(_stream_kernel, MOBA_BLOCK // page),
        grid_spec=grid_spec,
        out_shape=[jax.ShapeDtypeStruct((db, rows_b // 2, LANES), F32),
                   jax.ShapeDtypeStruct((db, rows_a, LANES), jnp.int32)],
        compiler_params=pltpu.CompilerParams(dimension_semantics=("arbitrary", "arbitrary"),
                                             vmem_limit_bytes=VMEM_LIMIT),
    )(page_table.reshape(-1), qb_rows, qa_rows, knew, vnew, bnew, bias, lam_params, subln,
      *([pool_kb] * pps), *([pool_vb] * pps), *([pool_ka_t] * pps))


def _moba_sample_kernel(ppb, last_block, t_new, pages_per_req, pt_ref, sel_ref, q_ref, knew_ref,
                        vnew_ref, bown_ref, bnear_ref, cfar_ref, pool_k_ref, pool_v_ref, o_ref,
                        kbuf, vbuf, sem):
    n_sel = ppb * MOBA_TOPK
    n_cp = t_new * n_sel
    n_heads = q_ref.shape[1]
    n_it = q_ref.shape[0] * n_heads // MOBA_PAIRS
    page = kbuf.shape[3]

    def page_copy(pool_ref, buf, kind, slot, jt, phys, h):
        return pltpu.make_async_copy(pool_ref.at[phys, h], buf.at[slot, jt], sem.at[slot, kind, jt])

    def start_all(it, slot):
        for pr in range(MOBA_PAIRS):
            g = it * MOBA_PAIRS + pr
            b, h = g // n_heads, g % n_heads
            for jt in range(n_cp):
                t, k, p = jt // n_sel, (jt % n_sel) // ppb, jt % ppb
                blk = sel_ref[(g * t_new + t) * MOBA_TOPK + k]
                phys = pt_ref[b * pages_per_req + blk * ppb + p]
                page_copy(pool_k_ref, kbuf, 0, slot, pr * n_cp + jt, phys, h).start()
                page_copy(pool_v_ref, vbuf, 1, slot, pr * n_cp + jt, phys, h).start()

    def wait_all(slot):
        for jt in range(MOBA_PAIRS * n_cp):
            page_copy(pool_k_ref, kbuf, 0, slot, jt, 0, 0).wait()
            page_copy(pool_v_ref, vbuf, 1, slot, jt, 0, 0).wait()

    row = lax.broadcasted_iota(jnp.int32, (8, page), 0)
    row_d = lax.broadcasted_iota(jnp.int32, (8, HEAD_DIM), 0)

    def score_phase(g, slot, base):
        b, h = g // n_heads, g % n_heads
        q8 = q_ref[b, h]
        bnear = bnear_ref[h]
        scores = []
        for j in range(n_sel):
            s_j = jnp.zeros((8, page), F32)
            for t in range(t_new):
                res = jnp.dot(q8, kbuf[slot, base + t * n_sel + j], preferred_element_type=F32)
                s_j = jnp.where(row == t, res, s_j)
            bias = jnp.broadcast_to(cfar_ref[h], (8, page))
            if j % ppb == ppb - 1:
                for t in range(t_new):
                    blk = sel_ref[(g * t_new + t) * MOBA_TOPK + j // ppb]
                    bias = jnp.where((row == t) & (blk == last_block), bnear, bias)
            scores.append(s_j + bias)
        s = jnp.concatenate(scores, axis=-1)
        knew = knew_ref[b, h]
        own = [jnp.sum(q8 * knew[t2:t2 + 1, :], axis=-1, keepdims=True) + bown_ref[h, t2]
               for t2 in range(t_new)]
        return b, h, s, own

    def softmax_phase(b, h, s, own):
        vnew = vnew_ref[b, h]
        m = jnp.max(s, axis=-1, keepdims=True)
        for c in own:
            m = jnp.maximum(m, c)
        p = jnp.exp(s - m)
        l = jnp.sum(p, axis=-1, keepdims=True)
        acc = jnp.zeros((8, HEAD_DIM), F32)
        for t2, c in enumerate(own):
            p_own = jnp.exp(c - m)
            l = l + p_own
            acc = acc + p_own * vnew[t2:t2 + 1, :]
        return p, l, acc

    def value_phase(p, l, acc, slot, base):
        for j in range(n_sel):
            for t in range(t_new):
                res = _nt(p[:, j * page:(j + 1) * page], vbuf[slot, base + t * n_sel + j])
                acc = acc + jnp.where(row_d == t, res, 0.0)
        return jnp.where(row_d < t_new, acc / l, 0.0)

    def body(it, carry):
        slot = it % 2
        wait_all(slot)
        pairs = range(MOBA_PAIRS)
        scored = [score_phase(it * MOBA_PAIRS + pr, slot, pr * n_cp) for pr in pairs]
        start_all(jnp.minimum(it + 1, n_it - 1), 1 - slot)
        soft = [softmax_phase(*sc) for sc in scored]
        outs = [value_phase(*soft[pr], slot, pr * n_cp) for pr in pairs]
        for (b, h, _, _), out in zip(scored, outs):
            o_ref[b, h] = out
        return carry

    start_all(0, 0)
    lax.fori_loop(0, n_it, body, 0)
    wait_all(n_it % 2)


def _moba_sample(page_table, sel, q8, knew8, vnew8, bown, bnear, cfar, pool_k_t, pool_v_t):
    db, n_pages = page_table.shape
    page = pool_k_t.shape[3]
    t_new = sel.shape[2]
    ppb = MOBA_BLOCK // page
    last_block = n_pages // ppb - 1
    n_cp = MOBA_PAIRS * t_new * MOBA_TOPK * ppb
    assert (db * MOBA_HEADS) % MOBA_PAIRS == 0
    whole = lambda a: pl.BlockSpec(a.shape, lambda i, pt, sl: (0,) * a.ndim)
    hbm = pl.BlockSpec(memory_space=pl.ANY)
    out_shape = jax.ShapeDtypeStruct((db, MOBA_HEADS, 8, HEAD_DIM), F32)
    grid_spec = pltpu.PrefetchScalarGridSpec(
        num_scalar_prefetch=2,
        grid=(1,),
        in_specs=[whole(q8), whole(knew8), whole(vnew8), whole(bown), whole(bnear), whole(cfar), hbm, hbm],
        out_specs=pl.BlockSpec(out_shape.shape, lambda i, pt, sl: (0, 0, 0, 0)),
        scratch_shapes=[pltpu.VMEM((2, n_cp, HEAD_DIM, page), F32),
                        pltpu.VMEM((2, n_cp, HEAD_DIM, page), F32),
                        pltpu.SemaphoreType.DMA((2, 2, n_cp))],
    )
    return pl.pallas_call(
        functools.partial(_moba_sample_kernel, ppb, last_block, t_new, n_pages),
        grid_spec=grid_spec,
        out_shape=out_shape,
        compiler_params=pltpu.CompilerParams(dimension_semantics=("arbitrary",),
                                             vmem_limit_bytes=VMEM_LIMIT),
    )(page_table.reshape(-1), sel.reshape(-1), q8, knew8, vnew8, bown, bnear, cfar, pool_k_t, pool_v_t)


def kernel(x_prompt, x_sample, cache_k_a, cache_v_a, cache_k_b, cache_v_b, page_table, g_mix, w_in,
           w_merge_gate, b_merge_gate, w_branch_a, w_branch_b, w_out, lambda_q1, lambda_k1, lambda_q2,
           lambda_k2, subln_w, g_ffn, w_ffn_up, w_ffn_down, rel_bias, g_final):
    batch, seq, d = x_prompt.shape
    db, t_new, _ = x_sample.shape
    depth, n_pool, page = cache_k_a.shape[:3]
    n_pages = page_table.shape[1]
    past = n_pages * page
    nb_past = past // MOBA_BLOCK
    w = ATT_WIDTH
    assert depth == 1 and seq % TQ == 0 and past % MOBA_BLOCK == 0 and MOBA_BLOCK % page == 0
    assert n_pages % PAGES_PER_STEP == 0 and MOBA_TOPK <= nb_past <= LANES and t_new <= 8
    assert page == LANES and PAGES_PER_STEP % (MOBA_BLOCK // page) == 0

    tab = rel_bias.astype(F32).T
    tab_a, tab_b = tab[:MOBA_HEADS], tab[MOBA_HEADS:]
    far_a, far_b = tab_a[:, REL_BUCKETS - 1], tab_b[:, REL_BUCKETS - 1]
    lam_params = jnp.concatenate([lambda_q1, lambda_k1, lambda_q2, lambda_k2], axis=0).astype(F32)
    subln = subln_w.astype(F32).reshape(1, 2 * HEAD_DIM)
    w_in_b = w_in[0].astype(BF16)
    merge_weights = (g_mix.reshape(1, d), w_merge_gate[0].astype(BF16), b_merge_gate.reshape(1, 2 * d),
                     w_branch_a[0].astype(BF16), w_branch_b[0].astype(BF16), w_out[0].astype(BF16),
                     g_ffn.reshape(1, d), w_ffn_up[0].astype(BF16), w_ffn_down[0].astype(BF16),
                     g_final.reshape(1, d))

    xp = x_prompt.reshape(batch * seq, d)
    scale = HEAD_DIM ** -0.5
    qk, ka_t, va_t, kb, vb, vat, vbt, kmean = _project(xp, g_mix, w_in_b, TQ, seq // TQ, scale * LOG2E)
    bias_a, cfar_a = _prompt_bias(tab_a, [(2 * u, 2 * u + 1) for u in range(UNITS)])
    bias_b, cfar_b = _prompt_bias(tab_b, [(u, u) for u in range(UNITS)])
    o_a = _prompt_attention(True, qk, vat, 0, batch, seq, bias_a, cfar_a,
                            (kmean.reshape(batch, seq // TQ, w),))
    o_b = _prompt_attention(False, qk, vbt, 2, batch, seq, bias_b, cfar_b,
                            (lam_params, subln.reshape(2 * HEAD_DIM, 1)))
    y_prompt = _merge_ffn(xp, o_a, o_b, merge_weights, MERGE_TM).reshape(batch, seq, d)

    n_s = db * t_new
    xs = x_sample.reshape(n_s, d)
    qk_s, ka_st, va_st, kb_s, vb_s, _, _, _ = _project(xs, g_mix, w_in_b, n_s, 1, scale)
    heads_first = lambda a: a.reshape(MOBA_HEADS, HEAD_DIM, db, t_new).transpose(2, 0, 3, 1)
    ka_s, va_s = heads_first(ka_st), heads_first(va_st)
    qk_s3 = qk_s.reshape(db, t_new, 4 * w).astype(F32)
    pad_to = lambda a, axis, n: jnp.pad(a, [(0, n - a.shape[i] if i == axis else 0) for i in range(a.ndim)])
    hb = DIFF_HEADS
    pool_kb = cache_k_b.reshape(n_pool, page * hb, 2 * HEAD_DIM)
    pool_vb = cache_v_b.reshape(n_pool, page * hb, 2 * HEAD_DIM)
    pool_ka_t = jnp.transpose(cache_k_a[0], (0, 2, 3, 1))
    pool_va_t = jnp.transpose(cache_v_a[0], (0, 2, 3, 1))

    lane = np.arange(LANES)
    qb16 = qk_s3[:, :, 2 * w:3 * w].reshape(db, t_new, hb, LANES).transpose(0, 2, 1, 3)
    qb16 = qb16.reshape(db, hb * t_new, LANES)
    qb_rows = jnp.concatenate([jnp.where(lane < HEAD_DIM, qb16, 0.0),
                               jnp.where(lane >= HEAD_DIM, qb16, 0.0)], axis=1)
    rows_b = 2 * hb * t_new
    head_of_row = (np.arange(rows_b) // t_new) % hb
    tok_of_row = np.arange(rows_b) % t_new
    head_mask = lambda cols: jnp.asarray(head_of_row[:, None] == (cols % hb)[None, :])
    tab_rows = tab_b[head_of_row]
    cols_new = np.arange(LANES)
    d_new = np.where(cols_new[None, :] < t_new * hb, tok_of_row[:, None] - (cols_new // hb)[None, :], -1)
    bnew = jnp.where(head_mask(cols_new), _bias_from_dist(tab_rows, d_new, per_row=True), NEG_INF)
    span = PAGES_PER_STEP * page
    cols = np.arange(span * hb)
    d_last = (past + tok_of_row)[:, None] - (past - span + cols // hb)[None, :]
    bias_s = jnp.stack([jnp.broadcast_to(far_b[head_of_row][:, None], (rows_b, span * hb)),
                        _bias_from_dist(tab_rows, d_last, per_row=True)])
    bias_s = jnp.where(head_mask(cols)[None], bias_s, NEG_INF)
    seg = np.arange(w) // HEAD_DIM
    qa_s = qk_s3[:, :, 0:w]
    qa_rows = jnp.where(jnp.asarray(seg[None, None, :] == np.arange(MOBA_HEADS)[:, None, None]),
                        qa_s[:, None], 0.0).reshape(db, MOBA_HEADS * t_new, w)
    knew_b = pad_to(kb_s.reshape(db, t_new * hb, 2 * HEAD_DIM), 1, LANES)
    vnew_b = pad_to(vb_s.reshape(db, t_new * hb, 2 * HEAD_DIM), 1, LANES)
    o_b16, sel_rows = _sample_stream(
        page_table, qb_rows, qa_rows, knew_b, vnew_b, bnew, bias_s, lam_params, subln,
        pool_kb, pool_vb, pool_ka_t.reshape(n_pool, w, page), page)
    o_b_s = o_b16.reshape(db, hb, t_new, LANES).transpose(0, 2, 1, 3).reshape(n_s, w).astype(BF16)
    sel = sel_rows[:, :, :MOBA_TOPK].reshape(db, MOBA_HEADS, t_new, MOBA_TOPK)

    q_heads = qa_s.reshape(db, t_new, MOBA_HEADS, HEAD_DIM).transpose(0, 2, 1, 3)
    d_own = np.arange(8)[None, :] - np.arange(t_new)[:, None]
    d_own = np.where(np.arange(8)[None, :] < t_new, d_own, -1)
    bown = _bias_from_dist(tab_a, d_own)[..., None]
    d_near = (past + np.arange(8))[:, None] - (past - page + np.arange(page))[None, :]
    bnear = _bias_from_dist(tab_a, d_near)
    cfar_s = jnp.broadcast_to(far_a[:, None, None], (MOBA_HEADS, 1, page))
    o_heads = _moba_sample(page_table, sel, pad_to(q_heads, 2, 8), pad_to(ka_s, 2, 8), pad_to(va_s, 2, 8),
                           bown, bnear, cfar_s, pool_ka_t, pool_va_t)
    o_a_s = o_heads[:, :, :t_new].transpose(0, 2, 1, 3).reshape(n_s, w).astype(BF16)
    y_sample = _merge_ffn(xs, o_a_s, o_b_s, merge_weights, n_s).reshape(db, t_new, d)

    a_shape = (MOBA_HEADS, HEAD_DIM)
    b_shape = (DIFF_HEADS, 2 * HEAD_DIM)
    rows_first = lambda a: a.reshape((batch,) + a_shape + (seq,)).transpose(0, 3, 1, 2)[None]
    return (y_prompt, y_sample,
            rows_first(ka_t), rows_first(va_t),
            kb.reshape((1, batch, seq) + b_shape), vb.reshape((1, batch, seq) + b_shape),
            ka_s.transpose(0, 2, 1, 3)[None], va_s.transpose(0, 2, 1, 3)[None],
            kb_s.reshape((1, db, t_new) + b_shape), vb_s.reshape((1, db, t_new) + b_shape))
```

```python
import functools
import math

import numpy as np
import jax
import jax.numpy as jnp
from jax import lax
from jax.experimental import pallas as pl
from jax.experimental.pallas import tpu as pltpu

HEAD_DIM = 64
MOBA_HEADS = 8
DIFF_HEADS = 4
MOBA_BLOCK = 256
MOBA_TOPK = 3
REL_BUCKETS = 32
REL_MAX_DIST = 128
RMS_EPS = 1e-6
LAMBDA_INIT = 0.8 - 0.6 * math.exp(-0.3 * 0)

LANES = 128
ATT_WIDTH = 512
UNITS = ATT_WIDTH // LANES
TQ = MOBA_BLOCK
SCORE_AHEAD = 2
MERGE_TM = 512
MOBA_PAIRS = 4
PAGES_PER_STEP = 16
VMEM_LIMIT = 56 * 1024 * 1024
BF16_ROWS = 16
V_ROWS = LANES + BF16_ROWS
LOG2E = math.log2(math.e)

F32 = jnp.float32
BF16 = jnp.bfloat16
NEG_INF = float("-inf")


def _rel_bucket_np(dist):
    n = np.maximum(dist, 0)
    exact = REL_BUCKETS // 2
    nf = np.maximum(n, 1).astype(np.float64)
    large = exact + (np.log(nf / exact) / math.log(REL_MAX_DIST / exact)
                     * (REL_BUCKETS - exact)).astype(np.int64)
    return np.where(n < exact, n, np.minimum(large, REL_BUCKETS - 1)).astype(np.int32)


def _bias_from_dist(tab, dist, per_row=False):
    idx = jnp.asarray(_rel_bucket_np(dist))
    col = (lambda k: tab[:, k:k + 1]) if per_row else (
        lambda k: tab[:, k].reshape((-1,) + (1,) * dist.ndim))
    b = jnp.broadcast_to(col(0), dist.shape if per_row else (tab.shape[0],) + dist.shape)
    for k in range(1, REL_BUCKETS):
        b = jnp.where(idx == k, col(k), b)
    return jnp.where(jnp.asarray(dist >= 0), b, NEG_INF)


def _nt(a, b, **kw):
    return lax.dot_general(a, b, (((1,), (1,)), ((), ())), preferred_element_type=F32, **kw)


def _const_spec(shape):
    n = len(shape)
    return pl.BlockSpec(shape, lambda *_: (0,) * n, pipeline_mode=pl.Buffered(1))


def _with_ones_rows(vt):
    row = lax.broadcasted_iota(jnp.int32, (BF16_ROWS, vt.shape[1]), 0)
    ones = jnp.where(row == 0, 1.0, 0.0).astype(BF16)
    parts = []
    for u in range(UNITS):
        parts += [vt[u * LANES:(u + 1) * LANES].astype(BF16), ones]
    return jnp.concatenate(parts, axis=0)


def _proj_kernel(scale, x_ref, g_ref, w_ref, qk_ref, kat_ref, vat_ref, kb_ref, vb_ref, vat16_ref,
                 vbt16_ref, kmean_ref):
    x = x_ref[...]
    xn = x * lax.rsqrt(jnp.mean(x * x, axis=-1, keepdims=True) + RMS_EPS) * g_ref[...]
    p = jnp.dot(xn.astype(BF16), w_ref[...], preferred_element_type=F32)
    w = ATT_WIDTH
    ka, va = p[:, w:2 * w], p[:, 2 * w:3 * w]
    kb, vb = p[:, 4 * w:5 * w], p[:, 5 * w:6 * w]
    vat = va.T
    kat_ref[0] = ka.T
    vat_ref[0] = vat
    for h in range(DIFF_HEADS):
        kb_ref[:, h, :] = kb[:, h * LANES:(h + 1) * LANES]
        vb_ref[:, h, :] = vb[:, h * LANES:(h + 1) * LANES]
    qk_ref[:, 0:w] = (p[:, 0:w] * scale).astype(BF16)
    qk_ref[:, w:2 * w] = ka.astype(BF16)
    qk_ref[:, 2 * w:3 * w] = (p[:, 3 * w:4 * w] * scale).astype(BF16)
    qk_ref[:, 3 * w:4 * w] = kb.astype(BF16)
    vat16_ref[0] = _with_ones_rows(vat)
    vbt16_ref[0] = _with_ones_rows(vb.T)
    kmean_ref[0] = jnp.mean(ka, axis=0, keepdims=True)


def _project(x2d, g, w_bf16, tm, tiles_per_seq, q_scale):
    n, d = x2d.shape
    w = ATT_WIDTH
    steps = n // tm
    seq = tm * tiles_per_seq
    row = lambda i: (i, 0)
    f32_rows = jax.ShapeDtypeStruct((n, DIFF_HEADS, 2 * HEAD_DIM), F32)
    rows_spec = pl.BlockSpec((tm, DIFF_HEADS, 2 * HEAD_DIM), lambda i: (i, 0, 0))
    f32_t = jax.ShapeDtypeStruct((n // seq, w, seq), F32)
    t_spec = pl.BlockSpec((1, w, tm), lambda i: (i // tiles_per_seq, 0, i % tiles_per_seq))
    vt16 = jax.ShapeDtypeStruct((steps, UNITS * V_ROWS, tm), BF16)
    return pl.pallas_call(
        functools.partial(_proj_kernel, q_scale),
        grid=(steps,),
        in_specs=[pl.BlockSpec((tm, d), row), _const_spec((1, d)), _const_spec((d, 6 * w))],
        out_specs=[pl.BlockSpec((tm, 4 * w), row), t_spec, t_spec,
                   rows_spec, rows_spec]
                  + [pl.BlockSpec((1, UNITS * V_ROWS, tm), lambda i: (i, 0, 0))] * 2
                  + [pl.BlockSpec((1, 1, w), lambda i: (i, 0, 0))],
        out_shape=[jax.ShapeDtypeStruct((n, 4 * w), BF16), f32_t, f32_t, f32_rows, f32_rows,
                   vt16, vt16, jax.ShapeDtypeStruct((steps, 1, w), F32)],
        compiler_params=pltpu.CompilerParams(dimension_semantics=("arbitrary",),
                                             vmem_limit_bytes=VMEM_LIMIT),
    )(x2d, g.reshape(1, d), w_bf16)


def _split_halves(q2):
    lane = lax.broadcasted_iota(jnp.int32, q2.shape, 1)
    zero = jnp.zeros_like(q2)
    return jnp.concatenate([jnp.where(lane < HEAD_DIM, q2, zero),
                            jnp.where(lane >= HEAD_DIM, q2, zero)], axis=0)


def _online_update(u, s, shift, vt, m_ref, acc_ref):
    m_old = m_ref[u]
    m_new = jnp.maximum(m_old, jnp.max(s, axis=0, keepdims=True) + shift)
    p = jnp.exp2(s - (m_new - shift))
    alpha = jnp.exp2(m_old - m_new)
    acc_ref[u] = alpha * acc_ref[u] + jnp.dot(vt, p.astype(BF16), preferred_element_type=F32)
    m_ref[u] = m_new


def _units_ahead(score_fn, fold_fn, first=(), after=None):
    pending = list(first)
    issued = len(pending)
    total = UNITS + (SCORE_AHEAD if after is not None else 0)
    for u in range(UNITS):
        while issued < min(u + SCORE_AHEAD + 1, total):
            pending.append(score_fn(issued) if issued < UNITS else after(issued - UNITS))
            issued += 1
        fold_fn(u, pending.pop(0))
    return tuple(pending)


def _attn_kernel(moba, *refs):
    if moba:
        (q_ref, k_ref, vt_ref, bias_ref, cfar_ref, kmean_ref, o_ref,
         qs_ref, m_ref, acc_ref, sh_ref) = refs
    else:
        (q_ref, k_ref, vt_ref, bias_ref, cfar_ref, lam_ref, subln_ref, o_ref,
         qs_ref, m_ref, acc_ref) = refs
    i = pl.program_id(1)
    tq = TQ

    def k_tile(n, u):
        return k_ref[pl.ds(pl.multiple_of(n * tq, tq), tq), u * LANES:(u + 1) * LANES]

    def vt_tile(n, u):
        return vt_ref[n, u * V_ROWS:(u + 1) * V_ROWS, :]

    def scores(n):
        return lambda u: _nt(k_tile(n, u), qs_ref[u])

    for u in range(UNITS):
        qs = _split_halves(q_ref[:, u * LANES:(u + 1) * LANES])
        qs_ref[u] = qs
        if moba:
            km = kmean_ref[0, :, u * LANES:(u + 1) * LANES]
            hi = km.astype(BF16)
            r1 = km - hi.astype(F32)
            mid = r1.astype(BF16)
            lo = (r1 - mid.astype(F32)).astype(BF16)
            g = _nt(hi, qs) + _nt(mid, qs) + _nt(lo, qs)
            nb = g.shape[0]
            blk = lax.broadcasted_iota(jnp.int32, g.shape, 0)
            valid = blk < i
            g = jnp.where(valid, g, NEG_INF)
            rank = jnp.zeros(g.shape, jnp.int32)
            for mm in range(nb):
                gm = g[mm:mm + 1, :]
                ahead = (gm > g) | ((gm == g) & (mm < blk))
                rank = rank + ahead.astype(jnp.int32)
            sel = valid & (rank < MOBA_TOPK)
            sh_ref[u] = jnp.where(sel, cfar_ref[u], NEG_INF)

    def fold_own(u, s):
        s = s + bias_ref[u, 0]
        m = jnp.max(s, axis=0, keepdims=True)
        p = jnp.exp2(s - m)
        m_ref[u] = m
        acc_ref[u] = jnp.dot(vt_tile(i, u), p.astype(BF16), preferred_element_type=F32)

    _units_ahead(scores(i), fold_own)

    @pl.when(i >= 1)
    def _():
        def fold_prev(u, s):
            if moba:
                shift = sh_ref[u, pl.ds(i - 1, 1), :] - cfar_ref[u]
            else:
                shift = jnp.zeros((1, 2 * tq), F32)
            _online_update(u, s + bias_ref[u, 1], shift, vt_tile(i - 1, u), m_ref, acc_ref)

        _units_ahead(scores(i - 1), fold_prev)

    def far(n, s_first):
        def fold_far(u, s):
            shift = sh_ref[u, pl.ds(n, 1), :] if moba else cfar_ref[u]
            _online_update(u, s, shift, vt_tile(n, u), m_ref, acc_ref)

        nxt = jnp.minimum(n + 1, i - 2)
        return _units_ahead(scores(n), fold_far, first=s_first, after=scores(nxt))

    lax.fori_loop(0, i - 1, far, tuple(scores(0)(k) for k in range(SCORE_AHEAD)))

    chan = lax.broadcasted_iota(jnp.int32, (LANES, tq), 0)
    if not moba:
        lp = lam_ref[...]
        lam = (jnp.exp(jnp.sum(lp[0:1] * lp[1:2], axis=-1, keepdims=True))
               - jnp.exp(jnp.sum(lp[2:3] * lp[3:4], axis=-1, keepdims=True)) + LAMBDA_INIT)
    for u in range(UNITS):
        acc = acc_ref[u]
        o = acc[:LANES] / acc[LANES:LANES + 1]
        if moba:
            out = jnp.where(chan < HEAD_DIM, o[:, :tq], o[:, tq:])
        else:
            out = o[:, :tq] - lam * o[:, tq:]
            out = out * lax.rsqrt(jnp.mean(out * out, axis=0, keepdims=True) + RMS_EPS)
            out = out * subln_ref[...] * (1.0 - LAMBDA_INIT)
        o_ref[:, u * LANES:(u + 1) * LANES] = out.T.astype(o_ref.dtype)


def _prompt_attention(moba, qk, vt, col0, batch, seq, bias, cfar, extra):
    nt = seq // TQ
    w = ATT_WIDTH
    in_specs = [
        pl.BlockSpec((TQ, w), lambda b, i: (b * nt + i, col0)),
        pl.BlockSpec((seq, w), lambda b, i: (b, col0 + 1)),
        pl.BlockSpec((nt, UNITS * V_ROWS, TQ), lambda b, i: (b, 0, 0)),
        _const_spec(bias.shape),
        _const_spec(cfar.shape),
    ]
    scratch = [pltpu.VMEM((UNITS, 2 * TQ, LANES), BF16),
               pltpu.VMEM((UNITS, 1, 2 * TQ), F32),
               pltpu.VMEM((UNITS, V_ROWS, 2 * TQ), F32)]
    if moba:
        (kmean,) = extra
        in_specs.append(pl.BlockSpec((1, nt, w), lambda b, i: (b, 0, 0)))
        scratch.append(pltpu.VMEM((UNITS, nt, 2 * TQ), F32))
    else:
        lam_params, subln_col = extra
        in_specs += [_const_spec(lam_params.shape), _const_spec(subln_col.shape)]
    return pl.pallas_call(
        functools.partial(_attn_kernel, moba),
        grid=(batch, nt),
        in_specs=in_specs,
        out_specs=pl.BlockSpec((TQ, w), lambda b, i: (b * nt + i, 0)),
        out_shape=jax.ShapeDtypeStruct((batch * seq, w), BF16),
        scratch_shapes=scratch,
        compiler_params=pltpu.CompilerParams(dimension_semantics=("arbitrary", "arbitrary"),
                                             vmem_limit_bytes=VMEM_LIMIT),
    )(qk, qk, vt, bias, cfar, *extra)


def _prompt_bias(tab, heads_of_unit):
    r = np.arange(TQ)[None, :] - np.arange(TQ)[:, None]
    own = _bias_from_dist(tab, r)
    prev = _bias_from_dist(tab, r + TQ)
    far = tab[:, REL_BUCKETS - 1]
    bias, cfar = [], []
    for h0, h1 in heads_of_unit:
        bias.append(jnp.stack([jnp.concatenate([own[h0], own[h1]], axis=1),
                               jnp.concatenate([prev[h0], prev[h1]], axis=1)]))
        cfar.append(jnp.concatenate([jnp.full((1, TQ), far[h0]), jnp.full((1, TQ), far[h1])], axis=1))
    return jnp.stack(bias) * LOG2E, jnp.stack(cfar) * LOG2E


def _merge_ffn_kernel(ffn_chunk, x_ref, oa_ref, ob_ref, gmix_ref, wg_ref, bg_ref, wa_ref, wb_ref,
                      wo_ref, gffn_ref, wup_ref, wdn_ref, gfin_ref, y_ref):
    x = x_ref[...]
    d = x.shape[1]
    xn = (x * lax.rsqrt(jnp.mean(x * x, axis=-1, keepdims=True) + RMS_EPS) * gmix_ref[...]).astype(BF16)
    gates = jax.nn.sigmoid(jnp.dot(xn, wg_ref[...], preferred_element_type=F32) + bg_ref[...])
    merged = (gates[:, :d] * jnp.dot(oa_ref[...], wa_ref[...], preferred_element_type=F32)
              + gates[:, d:] * jnp.dot(ob_ref[...], wb_ref[...], preferred_element_type=F32))
    h = x + jnp.dot(merged.astype(BF16), wo_ref[...], preferred_element_type=F32)
    hn = (h * lax.rsqrt(jnp.mean(h * h, axis=-1, keepdims=True) + RMS_EPS) * gffn_ref[...]).astype(BF16)
    hidden = wdn_ref.shape[0]
    out = h
    for c in range(hidden // ffn_chunk):
        lo = c * ffn_chunk
        gate = jnp.dot(hn, wup_ref[:, lo:lo + ffn_chunk], preferred_element_type=F32)
        up = jnp.dot(hn, wup_ref[:, hidden + lo:hidden + lo + ffn_chunk], preferred_element_type=F32)
        act = (gate * jax.nn.sigmoid(gate) * up).astype(BF16)
        out = out + jnp.dot(act, wdn_ref[lo:lo + ffn_chunk, :], preferred_element_type=F32)
    y = out * lax.rsqrt(jnp.mean(out * out, axis=-1, keepdims=True) + RMS_EPS) * gfin_ref[...]
    y_ref[...] = y


def _merge_ffn(x2d, oa, ob, weights, tm):
    n, d = x2d.shape
    row = lambda i: (i, 0)
    in_specs = [pl.BlockSpec((tm, d), row), pl.BlockSpec((tm, ATT_WIDTH), row),
                pl.BlockSpec((tm, ATT_WIDTH), row)] + [_const_spec(w.shape) for w in weights]
    hidden = weights[-2].shape[0]
    return pl.pallas_call(
        functools.partial(_merge_ffn_kernel, math.gcd(hidden, 2 * LANES)),
        grid=(n // tm,),
        in_specs=in_specs,
        out_specs=pl.BlockSpec((tm, d), row),
        out_shape=jax.ShapeDtypeStruct((n, d), F32),
        compiler_params=pltpu.CompilerParams(dimension_semantics=("arbitrary",),
                                             vmem_limit_bytes=VMEM_LIMIT),
    )(x2d, oa, ob, *weights)


def _top_lanes(g, count):
    lane = lax.broadcasted_iota(jnp.int32, g.shape, 1)
    out = jnp.zeros(g.shape, jnp.int32)
    for k in range(count):
        best = jnp.max(g, axis=-1, keepdims=True)
        idx = jnp.min(jnp.where(g == best, lane, LANES), axis=-1, keepdims=True)
        out = jnp.where(lane == k, idx, out)
        g = jnp.where(lane == idx, NEG_INF, g)
    return out


def _stream_kernel(ppb, pt_ref, qb_ref, qa_ref, knew_ref, vnew_ref, bnew_ref, bias_ref, lam_ref,
                   subln_ref, *refs):
    del pt_ref
    pps = PAGES_PER_STEP
    kb_refs, vb_refs, ka_refs = refs[:pps], refs[pps:2 * pps], refs[2 * pps:3 * pps]
    ob_ref, idx_ref, m_ref, l_ref, acc_ref, gate_ref = refs[3 * pps:]
    c = pl.program_id(1)
    nc = pl.num_programs(1)
    qb = qb_ref[0]
    half = qb.shape[0] // 2

    @pl.when(c == 0)
    def _():
        s = _nt(qb, knew_ref[0]) + bnew_ref[...]
        m = jnp.max(s, axis=-1, keepdims=True)
        p = jnp.exp(s - m)
        m_ref[...] = m
        l_ref[...] = jnp.sum(p, axis=-1, keepdims=True)
        acc_ref[...] = jnp.dot(p, vnew_ref[0], preferred_element_type=F32)
        gate_ref[...] = jnp.full(gate_ref.shape, NEG_INF, F32)

    s = jnp.concatenate([_nt(qb, kr[...]) for kr in kb_refs], axis=-1) + bias_ref[0]
    m_old = m_ref[...]
    m_new = jnp.maximum(m_old, jnp.max(s, axis=-1, keepdims=True))
    p = jnp.exp(s - m_new)
    alpha = jnp.exp(m_old - m_new)
    l_ref[...] = alpha * l_ref[...] + jnp.sum(p, axis=-1, keepdims=True)
    width = kb_refs[0].shape[0]
    pv = jnp.dot(p[:, :width], vb_refs[0][...], preferred_element_type=F32)
    for j in range(1, pps):
        pv = pv + jnp.dot(p[:, j * width:(j + 1) * width], vb_refs[j][...], preferred_element_type=F32)
    acc_ref[...] = alpha * acc_ref[...] + pv
    m_ref[...] = m_new

    qa = qa_ref[0]
    gate = gate_ref[...]
    lane = lax.broadcasted_iota(jnp.int32, gate.shape, 1)
    for j in range(pps // ppb):
        sa = jnp.dot(qa, ka_refs[j * ppb][...], preferred_element_type=F32)
        for jj in range(1, ppb):
            sa = sa + jnp.dot(qa, ka_refs[j * ppb + jj][...], preferred_element_type=F32)
        g = jnp.sum(sa, axis=-1, keepdims=True)
        gate = jnp.where(lane == c * (pps // ppb) + j, g, gate)
    gate_ref[...] = gate

    @pl.when(c == nc - 1)
    def _():
        lp = lam_ref[...]
        lam = (jnp.exp(jnp.sum(lp[0:1] * lp[1:2], axis=-1, keepdims=True))
               - jnp.exp(jnp.sum(lp[2:3] * lp[3:4], axis=-1, keepdims=True)) + LAMBDA_INIT)
        o = acc_ref[...] / l_ref[...]
        out = o[:half] - lam * o[half:]
        out = out * lax.rsqrt(jnp.mean(out * out, axis=-1, keepdims=True) + RMS_EPS)
        ob_ref[0] = out * subln_ref[...] * (1.0 - LAMBDA_INIT)
        idx_ref[0] = _top_lanes(gate, MOBA_TOPK)


def _sample_stream(page_table, qb_rows, qa_rows, knew, vnew, bnew, bias, lam_params, subln,
                   pool_kb, pool_vb, pool_ka_t, page):
    db, n_pages = page_table.shape
    pps = PAGES_PER_STEP
    nc = n_pages // pps
    rows_b, rows_a = qb_rows.shape[1], qa_rows.shape[1]
    blk_b, blk_a = pool_kb.shape[1:], pool_ka_t.shape[1:]

    def page_spec(j, blk):
        return pl.BlockSpec((None,) + blk, lambda b, c, pt: (pt[b * n_pages + c * pps + j], 0, 0))

    per_b = lambda b, c, pt: (b, 0, 0)
    fixed = lambda b, c, pt: (0, 0)
    in_specs = [
        pl.BlockSpec((1,) + qb_rows.shape[1:], per_b),
        pl.BlockSpec((1,) + qa_rows.shape[1:], per_b),
        pl.BlockSpec((1,) + knew.shape[1:], per_b),
        pl.BlockSpec((1,) + vnew.shape[1:], per_b),
        pl.BlockSpec(bnew.shape, fixed),
        pl.BlockSpec((1,) + bias.shape[1:], lambda b, c, pt: (jnp.where(c == nc - 1, 1, 0), 0, 0)),
        pl.BlockSpec(lam_params.shape, fixed),
        pl.BlockSpec(subln.shape, fixed),
    ] + [page_spec(j, blk_b) for j in range(pps)] * 2 + [page_spec(j, blk_a) for j in range(pps)]
    grid_spec = pltpu.PrefetchScalarGridSpec(
        num_scalar_prefetch=1,
        grid=(db, nc),
        in_specs=in_specs,
        out_specs=[pl.BlockSpec((1, rows_b // 2, LANES), per_b), pl.BlockSpec((1, rows_a, LANES), per_b)],
        scratch_shapes=[pltpu.VMEM((rows_b, 1), F32), pltpu.VMEM((rows_b, 1), F32),
                        pltpu.VMEM((rows_b, LANES), F32), pltpu.VMEM((rows_a, LANES), F32)],
    )
    return pl.pallas_call(
        functools.partial(_stream_kernel, MOBA_BLOCK // page),
        grid_spec=grid_spec,
        out_shape=[jax.ShapeDtypeStruct((db, rows_b // 2, LANES), F32),
                   jax.ShapeDtypeStruct((db, rows_a, LANES), jnp.int32)],
        compiler_params=pltpu.CompilerParams(dimension_semantics=("arbitrary", "arbitrary"),
                                             vmem_limit_bytes=VMEM_LIMIT),
    )(page_table.reshape(-1), qb_rows, qa_rows, knew, vnew, bnew, bias, lam_params, subln,
      *([pool_kb] * pps), *([pool_vb] * pps), *([pool_ka_t] * pps))


def _moba_sample_kernel(ppb, last_block, t_new, pages_per_req, pt_ref, sel_ref, q_ref, knew_ref,
                        vnew_ref, bown_ref, bnear_ref, cfar_ref, pool_k_ref, pool_v_ref, o_ref,
                        kbuf, vbuf, sem):
    n_sel = ppb * MOBA_TOPK
    n_cp = t_new * n_sel
    n_heads = q_ref.shape[1]
    n_it = q_ref.shape[0] * n_heads // MOBA_PAIRS
    page = kbuf.shape[3]

    def page_copy(pool_ref, buf, kind, slot, jt, phys, h):
        return pltpu.make_async_copy(pool_ref.at[phys, h], buf.at[slot, jt], sem.at[slot, kind, jt])

    def start_all(it, slot):
        for pr in range(MOBA_PAIRS):
            g = it * MOBA_PAIRS + pr
            b, h = g // n_heads, g % n_heads
            for jt in range(n_cp):
                t, k, p = jt // n_sel, (jt % n_sel) // ppb, jt % ppb
                blk = sel_ref[(g * t_new + t) * MOBA_TOPK + k]
                phys = pt_ref[b * pages_per_req + blk * ppb + p]
                page_copy(pool_k_ref, kbuf, 0, slot, pr * n_cp + jt, phys, h).start(priority=jt % 2)
                page_copy(pool_v_ref, vbuf, 1, slot, pr * n_cp + jt, phys, h).start(priority=1 - jt % 2)

    def wait_all(slot):
        for jt in range(MOBA_PAIRS * n_cp):
            page_copy(pool_k_ref, kbuf, 0, slot, jt, 0, 0).wait()
            page_copy(pool_v_ref, vbuf, 1, slot, jt, 0, 0).wait()

    row = lax.broadcasted_iota(jnp.int32, (8, page), 0)
    row_d = lax.broadcasted_iota(jnp.int32, (8, HEAD_DIM), 0)

    def score_phase(g, slot, base):
        b, h = g // n_heads, g % n_heads
        q8 = q_ref[b, h]
        bnear = bnear_ref[h]
        scores = []
        for j in range(n_sel):
            s_j = jnp.zeros((8, page), F32)
            for t in range(t_new):
                res = jnp.dot(q8, kbuf[slot, base + t * n_sel + j], preferred_element_type=F32)
                s_j = jnp.where(row == t, res, s_j)
            bias = jnp.broadcast_to(cfar_ref[h], (8, page))
            if j % ppb == ppb - 1:
                for t in range(t_new):
                    blk = sel_ref[(g * t_new + t) * MOBA_TOPK + j // ppb]
                    bias = jnp.where((row == t) & (blk == last_block), bnear, bias)
            scores.append(s_j + bias)
        s = jnp.concatenate(scores, axis=-1)
        knew = knew_ref[b, h]
        own = [jnp.sum(q8 * knew[t2:t2 + 1, :], axis=-1, keepdims=True) + bown_ref[h, t2]
               for t2 in range(t_new)]
        return b, h, s, own

    def softmax_phase(b, h, s, own):
        vnew = vnew_ref[b, h]
        m = jnp.max(s, axis=-1, keepdims=True)
        for c in own:
            m = jnp.maximum(m, c)
        p = jnp.exp(s - m)
        l = jnp.sum(p, axis=-1, keepdims=True)
        acc = jnp.zeros((8, HEAD_DIM), F32)
        for t2, c in enumerate(own):
            p_own = jnp.exp(c - m)
            l = l + p_own
            acc = acc + p_own * vnew[t2:t2 + 1, :]
        return p, l, acc

    def value_phase(p, l, acc, slot, base):
        for j in range(n_sel):
            for t in range(t_new):
                res = _nt(p[:, j * page:(j + 1) * page], vbuf[slot, base + t * n_sel + j])
                acc = acc + jnp.where(row_d == t, res, 0.0)
        return jnp.where(row_d < t_new, acc / l, 0.0)

    def body(it, carry):
        slot = it % 2
        wait_all(slot)
        pairs = range(MOBA_PAIRS)
        scored = [score_phase(it * MOBA_PAIRS + pr, slot, pr * n_cp) for pr in pairs]
        start_all(jnp.minimum(it + 1, n_it - 1), 1 - slot)
        soft = [softmax_phase(*sc) for sc in scored]
        outs = [value_phase(*soft[pr], slot, pr * n_cp) for pr in pairs]
        for (b, h, _, _), out in zip(scored, outs):
            o_ref[b, h] = out
        return carry

    start_all(0, 0)
    lax.fori_loop(0, n_it, body, 0)
    wait_all(n_it % 2)


def _moba_sample(page_table, sel, q8, knew8, vnew8, bown, bnear, cfar, pool_k_t, pool_v_t):
    db, n_pages = page_table.shape
    page = pool_k_t.shape[3]
    t_new = sel.shape[2]
    ppb = MOBA_BLOCK // page
    last_block = n_pages // ppb - 1
    n_cp = MOBA_PAIRS * t_new * MOBA_TOPK * ppb
    assert (db * MOBA_HEADS) % MOBA_PAIRS == 0
    whole = lambda a: pl.BlockSpec(a.shape, lambda i, pt, sl: (0,) * a.ndim)
    hbm = pl.BlockSpec(memory_space=pl.ANY)
    out_shape = jax.ShapeDtypeStruct((db, MOBA_HEADS, 8, HEAD_DIM), F32)
    grid_spec = pltpu.PrefetchScalarGridSpec(
        num_scalar_prefetch=2,
        grid=(1,),
        in_specs=[whole(q8), whole(knew8), whole(vnew8), whole(bown), whole(bnear), whole(cfar), hbm, hbm],
        out_specs=pl.BlockSpec(out_shape.shape, lambda i, pt, sl: (0, 0, 0, 0)),
        scratch_shapes=[pltpu.VMEM((2, n_cp, HEAD_DIM, page), F32),
                        pltpu.VMEM((2, n_cp, HEAD_DIM, page), F32),
                        pltpu.SemaphoreType.DMA((2, 2, n_cp))],
    )
    return pl.pallas_call(
        functools.partial(_moba_sample_kernel, ppb, last_block, t_new, n_pages),
        grid_spec=grid_spec,
        out_shape=out_shape,
        compiler_params=pltpu.CompilerParams(dimension_semantics=("arbitrary",),
                                             vmem_limit_bytes=VMEM_LIMIT),
    )(page_table.reshape(-1), sel.reshape(-1), q8, knew8, vnew8, bown, bnear, cfar, pool_k_t, pool_v_t)


def kernel(x_prompt, x_sample, cache_k_a, cache_v_a, cache_k_b, cache_v_b, page_table, g_mix, w_in,
           w_merge_gate, b_merge_gate, w_branch_a, w_branch_b, w_out, lambda_q1, lambda_k1, lambda_q2,
           lambda_k2, subln_w, g_ffn, w_ffn_up, w_ffn_down, rel_bias, g_final):
    batch, seq, d = x_prompt.shape
    db, t_new, _ = x_sample.shape
    depth, n_pool, page = cache_k_a.shape[:3]
    n_pages = page_table.shape[1]
    past = n_pages * page
    nb_past = past // MOBA_BLOCK
    w = ATT_WIDTH
    assert depth == 1 and seq % TQ == 0 and past % MOBA_BLOCK == 0 and MOBA_BLOCK % page == 0
    assert n_pages % PAGES_PER_STEP == 0 and MOBA_TOPK <= nb_past <= LANES and t_new <= 8
    assert page == LANES and PAGES_PER_STEP % (MOBA_BLOCK // page) == 0

    tab = rel_bias.astype(F32).T
    tab_a, tab_b = tab[:MOBA_HEADS], tab[MOBA_HEADS:]
    far_a, far_b = tab_a[:, REL_BUCKETS - 1], tab_b[:, REL_BUCKETS - 1]
    lam_params = jnp.concatenate([lambda_q1, lambda_k1, lambda_q2, lambda_k2], axis=0).astype(F32)
    subln = subln_w.astype(F32).reshape(1, 2 * HEAD_DIM)
    w_in_b = w_in[0].astype(BF16)
    merge_weights = (g_mix.reshape(1, d), w_merge_gate[0].astype(BF16), b_merge_gate.reshape(1, 2 * d),
                     w_branch_a[0].astype(BF16), w_branch_b[0].astype(BF16), w_out[0].astype(BF16),
                     g_ffn.reshape(1, d), w_ffn_up[0].astype(BF16), w_ffn_down[0].astype(BF16),
                     g_final.reshape(1, d))

    xp = x_prompt.reshape(batch * seq, d)
    scale = HEAD_DIM ** -0.5
    qk, ka_t, va_t, kb, vb, vat, vbt, kmean = _project(xp, g_mix, w_in_b, TQ, seq // TQ, scale * LOG2E)
    bias_a, cfar_a = _prompt_bias(tab_a, [(2 * u, 2 * u + 1) for u in range(UNITS)])
    bias_b, cfar_b = _prompt_bias(tab_b, [(u, u) for u in range(UNITS)])
    o_a = _prompt_attention(True, qk, vat, 0, batch, seq, bias_a, cfar_a,
                            (kmean.reshape(batch, seq // TQ, w),))
    o_b = _prompt_attention(False, qk, vbt, 2, batch, seq, bias_b, cfar_b,
                            (lam_params, subln.reshape(2 * HEAD_DIM, 1)))
    y_prompt = _merge_ffn(xp, o_a, o_b, merge_weights, MERGE_TM).reshape(batch, seq, d)

    n_s = db * t_new
    xs = x_sample.reshape(n_s, d)
    qk_s, ka_st, va_st, kb_s, vb_s, _, _, _ = _project(xs, g_mix, w_in_b, n_s, 1, scale)
    heads_first = lambda a: a.reshape(MOBA_HEADS, HEAD_DIM, db, t_new).transpose(2, 0, 3, 1)
    ka_s, va_s = heads_first(ka_st), heads_first(va_st)
    qk_s3 = qk_s.reshape(db, t_new, 4 * w).astype(F32)
    pad_to = lambda a, axis, n: jnp.pad(a, [(0, n - a.shape[i] if i == axis else 0) for i in range(a.ndim)])
    hb = DIFF_HEADS
    pool_kb = cache_k_b.reshape(n_pool, page * hb, 2 * HEAD_DIM)
    pool_vb = cache_v_b.reshape(n_pool, page * hb, 2 * HEAD_DIM)
    pool_ka_t = jnp.transpose(cache_k_a[0], (0, 2, 3, 1))
    pool_va_t = jnp.transpose(cache_v_a[0], (0, 2, 3, 1))

    lane = np.arange(LANES)
    qb16 = qk_s3[:, :, 2 * w:3 * w].reshape(db, t_new, hb, LANES).transpose(0, 2, 1, 3)
    qb16 = qb16.reshape(db, hb * t_new, LANES)
    qb_rows = jnp.concatenate([jnp.where(lane < HEAD_DIM, qb16, 0.0),
                               jnp.where(lane >= HEAD_DIM, qb16, 0.0)], axis=1)
    rows_b = 2 * hb * t_new
    head_of_row = (np.arange(rows_b) // t_new) % hb
    tok_of_row = np.arange(rows_b) % t_new
    head_mask = lambda cols: jnp.asarray(head_of_row[:, None] == (cols % hb)[None, :])
    tab_rows = tab_b[head_of_row]
    cols_new = np.arange(LANES)
    d_new = np.where(cols_new[None, :] < t_new * hb, tok_of_row[:, None] - (cols_new // hb)[None, :], -1)
    bnew = jnp.where(head_mask(cols_new), _bias_from_dist(tab_rows, d_new, per_row=True), NEG_INF)
    span = PAGES_PER_STEP * page
    cols = np.arange(span * hb)
    d_last = (past + tok_of_row)[:, None] - (past - span + cols // hb)[None, :]
    bias_s = jnp.stack([jnp.broadcast_to(far_b[head_of_row][:, None], (rows_b, span * hb)),
                        _bias_from_dist(tab_rows, d_last, per_row=True)])
    bias_s = jnp.where(head_mask(cols)[None], bias_s, NEG_INF)
    seg = np.arange(w) // HEAD_DIM
    qa_s = qk_s3[:, :, 0:w]
    qa_rows = jnp.where(jnp.asarray(seg[None, None, :] == np.arange(MOBA_HEADS)[:, None, None]),
                        qa_s[:, None], 0.0).reshape(db, MOBA_HEADS * t_new, w)
    knew_b = pad_to(kb_s.reshape(db, t_new * hb, 2 * HEAD_DIM), 1, LANES)
    vnew_b = pad_to(vb_s.reshape(db, t_new * hb, 2 * HEAD_DIM), 1, LANES)
    o_b16, sel_rows = _sample_stream(
        page_table, qb_rows, qa_rows, knew_b, vnew_b, bnew, bias_s, lam_params, subln,
        pool_kb, pool_vb, pool_ka_t.reshape(n_pool, w, page), page)
    o_b_s = o_b16.reshape(db, hb, t_new, LANES).transpose(0, 2, 1, 3).reshape(n_s, w).astype(BF16)
    sel = sel_rows[:, :, :MOBA_TOPK].reshape(db, MOBA_HEADS, t_new, MOBA_TOPK)

    q_heads = qa_s.reshape(db, t_new, MOBA_HEADS, HEAD_DIM).transpose(0, 2, 1, 3)
    d_own = np.arange(8)[None, :] - np.arange(t_new)[:, None]
    d_own = np.where(np.arange(8)[None, :] < t_new, d_own, -1)
    bown = _bias_from_dist(tab_a, d_own)[..., None]
    d_near = (past + np.arange(8))[:, None] - (past - page + np.arange(page))[None, :]
    bnear = _bias_from_dist(tab_a, d_near)
    cfar_s = jnp.broadcast_to(far_a[:, None, None], (MOBA_HEADS, 1, page))
    o_heads = _moba_sample(page_table, sel, pad_to(q_heads, 2, 8), pad_to(ka_s, 2, 8), pad_to(va_s, 2, 8),
                           bown, bnear, cfar_s, pool_ka_t, pool_va_t)
    o_a_s = o_heads[:, :, :t_new].transpose(0, 2, 1, 3).reshape(n_s, w).astype(BF16)
    y_sample = _merge_ffn(xs, o_a_s, o_b_s, merge_weights, n_s).reshape(db, t_new, d)

    a_shape = (MOBA_HEADS, HEAD_DIM)
    b_shape = (DIFF_HEADS, 2 * HEAD_DIM)
    rows_first = lambda a: a.reshape((batch,) + a_shape + (seq,)).transpose(0, 3, 1, 2)[None]
    return (y_prompt, y_sample,
            rows_first(ka_t), rows_first(va_t),
            kb.reshape((1, batch, seq) + b_shape), vb.reshape((1, batch, seq) + b_shape),
            ka_s.transpose(0, 2, 1, 3)[None], va_s.transpose(0, 2, 1, 3)[None],
            kb_s.reshape((1, db, t_new) + b_shape), vb_s.reshape((1, db, t_new) + b_shape))
```
